```python
import math
import jax, jax.numpy as jnp
from jax import lax
import numpy as np

D_MODEL = 1024
BATCH = 2
SEQ = 16384
DEPTH = 4

SSM_WIDTH = D_MODEL // 2
SSM_GROUP = 16
SSM_GROUPS = SSM_WIDTH // SSM_GROUP
SSM_STATE = 64
DT_MIN = 1e-3
DT_MAX = 1e-1
N_HEADS = 8
QK_NOPE = 64
QK_ROPE = 32
QK_DIM = QK_NOPE + QK_ROPE
V_DIM = 64
Q_LORA = 256
KV_LORA = 128
ROPE_BASE = 10000.0
Q_BLOCK = 128
D_FF = -(-8 * D_MODEL // (3 * 256)) * 256
N_MOD = 6
EPS = 1e-6
IN_SIZES = (SSM_WIDTH, Q_LORA, KV_LORA, QK_ROPE, D_MODEL, D_MODEL)
IN_COLS = SSM_WIDTH + Q_LORA + KV_LORA + QK_ROPE + 2 * D_MODEL

kernel_name = "hybrid_s5_mla_adaln_trunk"


def rms_norm(x, g):
    xf = x.astype(jnp.float32)
    y = xf * lax.rsqrt(jnp.mean(xf * xf, axis=-1, keepdims=True) + EPS)
    return (y * g.astype(jnp.float32)).astype(x.dtype)


def rope(x, cos, sin):
    x1, x2 = jnp.split(x, 2, axis=-1)
    return jnp.concatenate([x1 * cos - x2 * sin, x2 * cos + x1 * sin], axis=-1)


def _ssm_combine(left, right):
    a1r, a1i, b1r, b1i = left
    a2r, a2i, b2r, b2i = right
    return (a2r * a1r - a2i * a1i,
            a2r * a1i + a2i * a1r,
            a2r * b1r - a2i * b1i + b2r,
            a2r * b1i + a2i * b1r + b2i)


def s5_mixer(u, a_re, a_im, log_dt, b_re, b_im, c_re, c_im, d_skip, w_glu, b_glu):
    f32 = jnp.float32
    bsz, L, _ = u.shape
    dt = jnp.exp(log_dt.astype(f32))[:, None]
    ar = a_re.astype(f32)
    ai = a_im.astype(f32)
    mag = jnp.exp(ar * dt)
    abar_re = mag * jnp.cos(ai * dt)
    abar_im = mag * jnp.sin(ai * dt)
    den = ar * ar + ai * ai
    nr = abar_re - 1.0
    ni = abar_im
    coef_re = ((nr * ar + ni * ai) / den)[..., None]
    coef_im = ((ni * ar - nr * ai) / den)[..., None]
    br = b_re.astype(f32)
    bi = b_im.astype(f32)
    bbar_re = coef_re * br - coef_im * bi
    bbar_im = coef_re * bi + coef_im * br
    ug = u.reshape(bsz, L, SSM_GROUPS, SSM_GROUP).astype(f32)
    bu_re = jnp.einsum('blgm,gpm->blgp', ug, bbar_re)
    bu_im = jnp.einsum('blgm,gpm->blgp', ug, bbar_im)
    a_seq_re = jnp.broadcast_to(abar_re[None, None], (1, L, SSM_GROUPS, SSM_STATE))
    a_seq_im = jnp.broadcast_to(abar_im[None, None], (1, L, SSM_GROUPS, SSM_STATE))
    _, _, h_re, h_im = lax.associative_scan(
        _ssm_combine, (a_seq_re, a_seq_im, bu_re, bu_im), axis=1)
    y = (jnp.einsum('blgp,gmp->blgm', h_re, c_re.astype(f32))
         - jnp.einsum('blgp,gmp->blgm', h_im, c_im.astype(f32))
         + d_skip.astype(f32).reshape(SSM_GROUPS, SSM_GROUP) * ug)
    y = jax.nn.gelu(y.reshape(bsz, L, SSM_WIDTH)).astype(u.dtype)
    return y * jax.nn.sigmoid(y @ w_glu + b_glu)


def mla_mixer(cq_in, ckv_in, kr_in, positions, cos, sin, q_norm_g, w_uq, kv_norm_g, w_uk, w_uv):
    bsz, L, _ = cq_in.shape
    cq = rms_norm(cq_in, q_norm_g)
    q = (cq @ w_uq).reshape(bsz, L, N_HEADS, QK_DIM)
    q = jnp.concatenate([q[..., :QK_NOPE],
                         rope(q[..., QK_NOPE:], cos[:, :, None], sin[:, :, None])], axis=-1)
    ckv = rms_norm(ckv_in, kv_norm_g)
    k_nope = (ckv @ w_uk).reshape(bsz, L, N_HEADS, QK_NOPE)
    v = (ckv @ w_uv).reshape(bsz, L, N_HEADS, V_DIM)
    k_pe = rope(kr_in[:, :, None, :], cos[:, :, None], sin[:, :, None])
    k = jnp.concatenate([k_nope, jnp.broadcast_to(k_pe, (bsz, L, N_HEADS, QK_ROPE))], axis=-1)
    scale = QK_DIM ** -0.5
    nb = L // Q_BLOCK
    qb = q.reshape(bsz, nb, Q_BLOCK, N_HEADS, QK_DIM).transpose(1, 0, 2, 3, 4)
    pb = positions.reshape(bsz, nb, Q_BLOCK).transpose(1, 0, 2)
    neg = jnp.finfo(jnp.float32).min

    def attend_block(args):
        qi, pi = args
        s = jnp.einsum('bqhd,bkhd->bhqk', qi, k).astype(jnp.float32) * scale
        mask = positions[:, None, None, :] <= pi[:, None, :, None]
        p = jax.nn.softmax(jnp.where(mask, s, neg), axis=-1).astype(v.dtype)
        return jnp.einsum('bhqk,bkhd->bqhd', p, v)

    o = lax.map(attend_block, (qb, pb))
    return o.transpose(1, 0, 2, 3, 4).reshape(bsz, L, N_HEADS * V_DIM)


def setup_inputs(seed: int = 0) -> dict:
    key = jax.random.key(seed)
    ks = jax.random.split(key, 32)
    nrm = jax.random.normal
    Lr = DEPTH
    G, P, M = SSM_GROUPS, SSM_STATE, SSM_GROUP
    x = nrm(ks[0], (BATCH, SEQ, D_MODEL), jnp.float32)
    c = nrm(ks[1], (BATCH, D_MODEL), jnp.float32)
    offset = jax.random.randint(ks[2], (BATCH, 1), 0, 4096, dtype=jnp.int32)
    positions = offset + jnp.arange(SEQ, dtype=jnp.int32)[None, :]
    n_idx = jnp.arange(P, dtype=jnp.float32)
    inv2 = 0.5 ** 0.5
    return {
        "x": x,
        "c": c,
        "positions": positions,
        "w_ada": nrm(ks[3], (Lr, D_MODEL, N_MOD * D_MODEL)) * (0.5 * D_MODEL ** -0.5),
        "b_ada": nrm(ks[4], (Lr, N_MOD * D_MODEL)) * 0.01,
        "norm1_g": 1.0 + 0.01 * nrm(ks[5], (Lr, D_MODEL)),
        "w_in": nrm(ks[6], (Lr, D_MODEL, IN_COLS)) * D_MODEL ** -0.5,
        "ssm_a_re": -0.5 * jnp.exp(0.02 * nrm(ks[7], (Lr, G, P))),
        "ssm_a_im": math.pi * n_idx[None, None, :] + 0.01 * nrm(ks[8], (Lr, G, P)),
        "ssm_log_dt": jax.random.uniform(ks[9], (Lr, G), jnp.float32, math.log(DT_MIN), math.log(DT_MAX)),
        "ssm_b_re": nrm(ks[10], (Lr, G, P, M)) * (inv2 * M ** -0.5),
        "ssm_b_im": nrm(ks[11], (Lr, G, P, M)) * (inv2 * M ** -0.5),
        "ssm_c_re": nrm(ks[12], (Lr, G, M, P)) * (inv2 * P ** -0.5),
        "ssm_c_im": nrm(ks[13], (Lr, G, M, P)) * (inv2 * P ** -0.5),
        "ssm_d": nrm(ks[14], (Lr, SSM_WIDTH)),
        "w_glu": nrm(ks[15], (Lr, SSM_WIDTH, SSM_WIDTH)) * SSM_WIDTH ** -0.5,
        "b_glu": nrm(ks[16], (Lr, SSM_WIDTH)) * 0.01,
        "w_a_out": nrm(ks[17], (Lr, SSM_WIDTH, D_MODEL)) * SSM_WIDTH ** -0.5,
        "q_norm_g": 1.0 + 0.01 * nrm(ks[18], (Lr, Q_LORA)),
        "w_uq": nrm(ks[19], (Lr, Q_LORA, N_HEADS * QK_DIM)) * Q_LORA ** -0.5,
        "kv_norm_g": 1.0 + 0.01 * nrm(ks[20], (Lr, KV_LORA)),
        "w_uk": nrm(ks[21], (Lr, KV_LORA, N_HEADS * QK_NOPE)) * KV_LORA ** -0.5,
        "w_uv": nrm(ks[22], (Lr, KV_LORA, N_HEADS * V_DIM)) * KV_LORA ** -0.5,
        "w_b_out": nrm(ks[23], (Lr, N_HEADS * V_DIM, D_MODEL)) * (N_HEADS * V_DIM) ** -0.5,
        "w_out": nrm(ks[24], (Lr, D_MODEL, D_MODEL)) * D_MODEL ** -0.5,
        "norm2_g": 1.0 + 0.01 * nrm(ks[25], (Lr, D_MODEL)),
        "w_gate": nrm(ks[26], (Lr, D_MODEL, D_FF)) * D_MODEL ** -0.5,
        "w_up": nrm(ks[27], (Lr, D_MODEL, D_FF)) * D_MODEL ** -0.5,
        "w_down": nrm(ks[28], (Lr, D_FF, D_MODEL)) * D_FF ** -0.5,
        "final_g": 1.0 + 0.01 * nrm(ks[29], (D_MODEL,)),
    }


def reference(x, c, positions, w_ada, b_ada, norm1_g, w_in, ssm_a_re, ssm_a_im, ssm_log_dt,
              ssm_b_re, ssm_b_im, ssm_c_re, ssm_c_im, ssm_d, w_glu, b_glu, w_a_out,
              q_norm_g, w_uq, kv_norm_g, w_uk, w_uv, w_b_out, w_out, norm2_g,
              w_gate, w_up, w_down, final_g):
    inv_freq = ROPE_BASE ** (-jnp.arange(0, QK_ROPE, 2, dtype=jnp.float32) / QK_ROPE)
    ang = positions.astype(jnp.float32)[..., None] * inv_freq
    cos = jnp.cos(ang).astype(x.dtype)
    sin = jnp.sin(ang).astype(x.dtype)
    split_pts = [int(s) for s in np.cumsum(IN_SIZES)[:-1]]
    c_act = jax.nn.silu(c)
    for l in range(DEPTH):
        mod = (c_act @ w_ada[l] + b_ada[l])[:, None, :]
        sh1, sc1, g1, sh2, sc2, g2 = jnp.split(mod, N_MOD, axis=-1)
        h = rms_norm(x, norm1_g[l]) * (1.0 + sc1) + sh1
        z = h @ w_in[l]
        u, cq_in, ckv_in, kr_in, gate_a, gate_b = jnp.split(z, split_pts, axis=-1)
        y_a = s5_mixer(u, ssm_a_re[l], ssm_a_im[l], ssm_log_dt[l], ssm_b_re[l], ssm_b_im[l],
                       ssm_c_re[l], ssm_c_im[l], ssm_d[l], w_glu[l], b_glu[l]) @ w_a_out[l]
        y_b = mla_mixer(cq_in, ckv_in, kr_in, positions, cos, sin, q_norm_g[l], w_uq[l],
                        kv_norm_g[l], w_uk[l], w_uv[l]) @ w_b_out[l]
        merged = jax.nn.sigmoid(gate_a) * y_a + jax.nn.sigmoid(gate_b) * y_b
        x = x + g1 * (merged @ w_out[l])
        h = rms_norm(x, norm2_g[l]) * (1.0 + sc2) + sh2
        x = x + g2 * ((jax.nn.silu(h @ w_gate[l]) * (h @ w_up[l])) @ w_down[l])
    return rms_norm(x, final_g)
```

```python
import functools
import math

import jax
import jax.numpy as jnp
import numpy as np
from jax import lax
from jax.experimental import pallas as pl
from jax.experimental.pallas import tpu as pltpu

D_MODEL = 1024
BATCH = 2
SEQ = 16384
DEPTH = 4
TOKENS = BATCH * SEQ

SSM_WIDTH = D_MODEL // 2
SSM_GROUP = 16
SSM_GROUPS = SSM_WIDTH // SSM_GROUP
SSM_STATE = 64
N_HEADS = 8
QK_NOPE = 64
QK_ROPE = 32
QK_DIM = QK_NOPE + QK_ROPE
V_DIM = 64
Q_LORA = 256
KV_LORA = 128
ROPE_BASE = 10000.0
D_FF = 2816
N_MOD = 6
EPS = 1e-6

LANES = 128
HEAD_PAD = LANES
SSM_CHUNK = 16
GROUPS_PER_TILE = LANES // SSM_GROUP
SSM_QBLOCKS = SSM_WIDTH // LANES
STATE_LANES = GROUPS_PER_TILE * SSM_STATE
CHUNK_LANES = SSM_CHUNK * LANES
VMEM_LIMIT = 56 * 1024 * 1024

F32 = jnp.float32
BF16 = jnp.bfloat16
NEG_BIG = -1e30


def _cparams(n_axes, vmem=VMEM_LIMIT):
    return pltpu.CompilerParams(dimension_semantics=("arbitrary",) * n_axes, vmem_limit_bytes=vmem)


def _sigmoid(x):
    return 1.0 / (1.0 + jnp.exp(-x))


def _split_bf16(a):
    hi = a.astype(BF16)
    lo = (a - hi.astype(F32)).astype(BF16)
    return hi, lo


def _dot(a, b):
    return jnp.dot(a, b, preferred_element_type=F32)


def _dot_nt(a, b):
    return lax.dot_general(a, b, (((1,), (1,)), ((), ())), preferred_element_type=F32)


def _dot3(a, b, nt=False):
    f = _dot_nt if nt else _dot
    ah, al = _split_bf16(a)
    bh, bl = _split_bf16(b)
    return f(ah, bh) + (f(al, bh) + f(ah, bl))


def _mod_kernel(c_ref, w_ref, b_ref, o_ref):
    c = c_ref[...]
    c_act = c * _sigmoid(c)
    o_ref[0] = _dot3(c_act, w_ref[0]) + b_ref[0]


def _adaln_mods(c_pad, w_ada, b_ada):
    ncol = N_MOD * D_MODEL
    blk = D_MODEL
    return pl.pallas_call(
        _mod_kernel,
        grid=(DEPTH, ncol // blk),
        in_specs=[
            pl.BlockSpec((8, D_MODEL), lambda l, j: (0, 0)),
            pl.BlockSpec((1, D_MODEL, blk), lambda l, j: (l, 0, j)),
            pl.BlockSpec((1, 1, blk), lambda l, j: (l, 0, j)),
        ],
        out_specs=pl.BlockSpec((1, 8, blk), lambda l, j: (l, 0, j)),
        out_shape=jax.ShapeDtypeStruct((DEPTH, 8, ncol), F32),
        compiler_params=_cparams(2),
        name="adaln_mod",
    )(c_pad, w_ada, b_ada.reshape(DEPTH, 1, ncol))


def _rope_kernel(pos_ref, freq_ref, sign_ref, c_ref, s_ref):
    ang = pos_ref[0].astype(F32) * freq_ref[...]
    c_ref[0] = jnp.cos(ang)
    s_ref[0] = jnp.sin(ang) * sign_ref[...]


def _rope_tables(positions, freq_lanes, sign_lanes):
    tm = 2048
    spec = pl.BlockSpec((1, tm, LANES), lambda b, i: (b, i, 0))
    vec = pl.BlockSpec((1, LANES), lambda b, i: (0, 0))
    return pl.pallas_call(
        _rope_kernel,
        grid=(BATCH, SEQ // tm),
        in_specs=[pl.BlockSpec((1, tm, 1), lambda b, i: (b, i, 0)), vec, vec],
        out_specs=[spec, spec],
        out_shape=[jax.ShapeDtypeStruct((BATCH, SEQ, LANES), F32)] * 2,
        compiler_params=_cparams(2),
        name="rope_tables",
    )(positions.reshape(BATCH, SEQ, 1), freq_lanes, sign_lanes)


def _ssm_prep_kernel(ar_ref, ai_ref, ldt_ref, bre_ref, bim_ref, cre_ref, cim_ref,
                     m_ref, ws_ref, wot_ref, dec_ref):
    ar = ar_ref[0, 0]
    ai = ai_ref[0, 0]
    dt = jnp.exp(ldt_ref[0, 0])
    mag = jnp.exp(ar * dt)
    ab_re = mag * jnp.cos(ai * dt)
    ab_im = mag * jnp.sin(ai * dt)
    den = ar * ar + ai * ai
    nr = ab_re - 1.0
    ni = ab_im
    coef_re = (nr * ar + ni * ai) / den
    coef_im = (ni * ar - nr * ai) / den
    b_re = bre_ref[0, 0]
    b_im = bim_ref[0, 0]
    bb_re = coef_re * b_re - coef_im * b_im
    bb_im = coef_re * b_im + coef_im * b_re
    c_re = cre_ref[0, 0]
    c_im = cim_ref[0, 0]

    pw = [(jnp.ones_like(ab_re), jnp.zeros_like(ab_im))]
    for _ in range(SSM_CHUNK):
        pr, pi = pw[-1]
        pw.append((pr * ab_re - pi * ab_im, pr * ab_im + pi * ab_re))

    bb_cat = jnp.concatenate([bb_re, -bb_im], axis=1)
    r0 = []
    for j in range(SSM_CHUNK + 1):
        pr, pi = pw[j]
        e_re = c_re * pr - c_im * pi
        e_im = c_re * pi + c_im * pr
        if j < SSM_CHUNK:
            r0.append(_dot3(bb_cat, jnp.concatenate([e_re, e_im], axis=1), nt=True).astype(BF16))
            s = SSM_CHUNK - 1 - j
            ws_ref[0, 0, s * LANES:(s + 1) * LANES, :] = jnp.concatenate(
                [bb_re * pr - bb_im * pi, bb_re * pi + bb_im * pr], axis=1).astype(BF16)
        if j >= 1:
            wot_ref[0, 0, (j - 1) * LANES:j * LANES, :] = jnp.concatenate([e_re, -e_im], axis=1).astype(BF16)
    zero = jnp.zeros((LANES, LANES), BF16)
    for s in range(SSM_CHUNK):
        for j in range(SSM_CHUNK):
            m_ref[0, 0, s * LANES:(s + 1) * LANES, j * LANES:(j + 1) * LANES] = r0[j - s] if j >= s else zero
    pr, pi = pw[SSM_CHUNK]
    dec_ref[0, 0] = jnp.concatenate([pr, pi], axis=1)


def _ssm_prepare(a_re, a_im, log_dt, b_re, b_im, c_re, c_im):
    nq, gt, p, m = SSM_QBLOCKS, GROUPS_PER_TILE, SSM_STATE, SSM_GROUP
    eye = jnp.eye(gt, dtype=F32)

    def lanes(v):
        return v.reshape(DEPTH, nq, 1, gt * p)

    def place(v):
        rows = v.shape[3]
        out = v[:, :, :, :, None, :] * eye[None, None, :, None, :, None]
        return out.reshape(DEPTH, nq, gt * rows, gt * p)

    ldt = jnp.broadcast_to(log_dt[:, :, None], (DEPTH, SSM_GROUPS, p))
    bt_re = place(jnp.swapaxes(b_re, 2, 3).reshape(DEPTH, nq, gt, m, p))
    bt_im = place(jnp.swapaxes(b_im, 2, 3).reshape(DEPTH, nq, gt, m, p))
    ct_re = place(c_re.reshape(DEPTH, nq, gt, m, p))
    ct_im = place(c_im.reshape(DEPTH, nq, gt, m, p))

    vec = pl.BlockSpec((1, 1, 1, STATE_LANES), lambda l, q: (l, q, 0, 0))
    mat = pl.BlockSpec((1, 1, LANES, STATE_LANES), lambda l, q: (l, q, 0, 0))
    return pl.pallas_call(
        _ssm_prep_kernel,
        grid=(DEPTH, nq),
        in_specs=[vec, vec, vec, mat, mat, mat, mat],
        out_specs=[
            pl.BlockSpec((1, 1, CHUNK_LANES, CHUNK_LANES), lambda l, q: (l, q, 0, 0)),
            pl.BlockSpec((1, 1, CHUNK_LANES, 2 * STATE_LANES), lambda l, q: (l, q, 0, 0)),
            pl.BlockSpec((1, 1, CHUNK_LANES, 2 * STATE_LANES), lambda l, q: (l, q, 0, 0)),
            pl.BlockSpec((1, 1, 1, 2 * STATE_LANES), lambda l, q: (l, q, 0, 0)),
        ],
        out_shape=[
            jax.ShapeDtypeStruct((DEPTH, nq, CHUNK_LANES, CHUNK_LANES), BF16),
            jax.ShapeDtypeStruct((DEPTH, nq, CHUNK_LANES, 2 * STATE_LANES), BF16),
            jax.ShapeDtypeStruct((DEPTH, nq, CHUNK_LANES, 2 * STATE_LANES), BF16),
            jax.ShapeDtypeStruct((DEPTH, nq, 1, 2 * STATE_LANES), F32),
        ],
        compiler_params=_cparams(2),
        name="ssm_prepare",
    )(lanes(a_re), lanes(a_im), lanes(ldt), bt_re, bt_im, ct_re, ct_im)


SSM_ROWS = 256
SSM_TOK = SSM_ROWS * SSM_CHUNK


def _ssm_kernel(u_ref, m_ref, ws_ref, wot_ref, dec_ref, d_ref, y_ref, s_scr, h_scr, carry_scr):
    r = pl.program_id(2)

    @pl.when(r == 0)
    def _():
        carry_scr[...] = jnp.zeros_like(carry_scr)

    x = jnp.concatenate([u_ref[pl.ds(s, SSM_ROWS, stride=SSM_CHUNK), :] for s in range(SSM_CHUNK)], axis=1)
    xb = x.astype(BF16)
    s_scr[...] = _dot(xb, ws_ref[0, 0])

    dec = dec_ref[0, 0]
    d_re = dec[:, :STATE_LANES]
    d_im = dec[:, STATE_LANES:]

    def body(i, h):
        base = pl.multiple_of(i * 8, 8)
        blk = s_scr[pl.ds(base, 8), :]
        rows = []
        for k in range(8):
            rows.append(h)
            h_re = h[:, :STATE_LANES]
            h_im = h[:, STATE_LANES:]
            inc = blk[k:k + 1, :]
            n_re = d_re * h_re - d_im * h_im + inc[:, :STATE_LANES]
            n_im = d_re * h_im + d_im * h_re + inc[:, STATE_LANES:]
            h = jnp.concatenate([n_re, n_im], axis=1)
        h_scr[pl.ds(base, 8), :] = jnp.concatenate(rows, axis=0)
        return h

    carry_scr[...] = lax.fori_loop(0, SSM_ROWS // 8, body, carry_scr[...])

    d_skip = jnp.concatenate([d_ref[0, 0]] * SSM_CHUNK, axis=1)
    y = _dot(xb, m_ref[0, 0]) + _dot_nt(h_scr[...].astype(BF16), wot_ref[0, 0]) + d_skip * x
    for j in range(SSM_CHUNK):
        y_ref[pl.ds(j, SSM_ROWS, stride=SSM_CHUNK), :] = y[:, j * LANES:(j + 1) * LANES]


def _ssm_scan(u, m_mat, ws, wot, dec, d_skip, layer):
    nr = SEQ // SSM_TOK
    tok = pl.BlockSpec((SSM_TOK, LANES), lambda q, b, r: (b * nr + r, q))
    return pl.pallas_call(
        _ssm_kernel,
        grid=(SSM_QBLOCKS, BATCH, nr),
        in_specs=[
            tok,
            pl.BlockSpec((1, 1, CHUNK_LANES, CHUNK_LANES), lambda q, b, r: (layer, q, 0, 0)),
            pl.BlockSpec((1, 1, CHUNK_LANES, 2 * STATE_LANES), lambda q, b, r: (layer, q, 0, 0)),
            pl.BlockSpec((1, 1, CHUNK_LANES, 2 * STATE_LANES), lambda q, b, r: (layer, q, 0, 0)),
            pl.BlockSpec((1, 1, 1, 2 * STATE_LANES), lambda q, b, r: (layer, q, 0, 0)),
            pl.BlockSpec((1, 1, 1, LANES), lambda q, b, r: (layer, q, 0, 0)),
        ],
        out_specs=tok,
        out_shape=jax.ShapeDtypeStruct((TOKENS, SSM_WIDTH), F32),
        scratch_shapes=[
            pltpu.VMEM((SSM_ROWS, 2 * STATE_LANES), F32),
            pltpu.VMEM((SSM_ROWS, 2 * STATE_LANES), F32),
            pltpu.VMEM((1, 2 * STATE_LANES), F32),
        ],
        compiler_params=_cparams(3),
        name="ssm_scan",
    )(u, m_mat, ws, wot, dec, d_skip)


IN_TM = 512
_U0, _CQ0, _CKV0, _KA0, _KB0, _GA0, _GB0, _IN_END = 0, 512, 768, 896, 1024, 1152, 2176, 3200


def _in_kernel(x_ref, mod_ref, g1_ref, w_ref, qg_ref, kvg_ref, wq1_ref, wq2_ref, wk_ref, wv_ref,
               c_ref, s_ref, u_ref, q_ref, k_ref, v_ref, ga_ref, gb_ref, *, q_scale):
    x = x_ref[...]
    mod = mod_ref[0]
    sh = mod[:, 0:D_MODEL]
    sc = mod[:, D_MODEL:2 * D_MODEL]
    xn = x * lax.rsqrt(jnp.mean(x * x, axis=-1, keepdims=True) + EPS) * g1_ref[...]
    h = (xn * (1.0 + sc) + sh).astype(BF16)
    z = _dot(h, w_ref[...])

    u_ref[...] = z[:, _U0:_CQ0]
    ga_ref[...] = _sigmoid(z[:, _GA0:_GB0])
    gb_ref[...] = _sigmoid(z[:, _GB0:_IN_END])

    cos = c_ref[0]
    sin = s_ref[0]
    cq = z[:, _CQ0:_CKV0]
    cqn = (cq * lax.rsqrt(jnp.mean(cq * cq, axis=-1, keepdims=True) + EPS) * qg_ref[...]).astype(BF16)
    qa = _dot(cqn, wq1_ref[...])
    qb = _dot(cqn, wq2_ref[...])
    ckv = z[:, _CKV0:_KA0]
    ckvn = (ckv * lax.rsqrt(jnp.mean(ckv * ckv, axis=-1, keepdims=True) + EPS) * kvg_ref[...]).astype(BF16)
    kn = _dot(ckvn, wk_ref[...])
    vv = _dot(ckvn, wv_ref[...])
    k_pe = z[:, _KA0:_KB0] * cos + z[:, _KB0:_GA0] * sin
    for hd in range(N_HEADS):
        sl = slice(hd * HEAD_PAD, (hd + 1) * HEAD_PAD)
        q_ref[0, hd] = ((qa[:, sl] * cos + qb[:, sl] * sin) * q_scale).astype(BF16)
        k_ref[0, hd] = (kn[:, sl] + k_pe).astype(BF16)
        v_ref[0, hd] = vv[:, sl].astype(BF16)


def _in_proj(x, mod_l, g1, w_all, qg, kvg, wq1, wq2, wk, wv, cos_t, sin_t):
    tm = IN_TM
    nt = SEQ // tm
    const = lambda shape: pl.BlockSpec(shape, lambda i: (0,) * len(shape))
    tok = lambda width: pl.BlockSpec((tm, width), lambda i: (i, 0))
    head = pl.BlockSpec((1, N_HEADS, tm, HEAD_PAD), lambda i: (i // nt, 0, i % nt, 0))
    tab = pl.BlockSpec((1, tm, LANES), lambda i: (i // nt, i % nt, 0))
    hshape = jax.ShapeDtypeStruct((BATCH, N_HEADS, SEQ, HEAD_PAD), BF16)
    q_scale = QK_DIM ** -0.5 * math.log2(math.e)
    return pl.pallas_call(
        functools.partial(_in_kernel, q_scale=q_scale),
        grid=(TOKENS // tm,),
        in_specs=[
            tok(D_MODEL),
            pl.BlockSpec((1, 1, N_MOD * D_MODEL), lambda i: (i // nt, 0, 0)),
            const((1, D_MODEL)),
            const((D_MODEL, _IN_END)),
            const((1, Q_LORA)), const((1, KV_LORA)),
            const((Q_LORA, N_HEADS * HEAD_PAD)), const((Q_LORA, N_HEADS * HEAD_PAD)),
            const((KV_LORA, N_HEADS * HEAD_PAD)), const((KV_LORA, N_HEADS * HEAD_PAD)),
            tab, tab,
        ],
        out_specs=[tok(SSM_WIDTH), head, head, head, tok(D_MODEL), tok(D_MODEL)],
        out_shape=[
            jax.ShapeDtypeStruct((TOKENS, SSM_WIDTH), F32), hshape, hshape, hshape,
            jax.ShapeDtypeStruct((TOKENS, D_MODEL), F32), jax.ShapeDtypeStruct((TOKENS, D_MODEL), F32),
        ],
        compiler_params=_cparams(1),
        name="in_proj",
    )(x, mod_l, g1, w_all, qg, kvg, wq1, wq2, wk, wv, cos_t, sin_t)


ATT_TQ = 512
ATT_TK = 512


def _attn_kernel(q_ref, k_ref, v_ref, pq_ref, pk_ref, o_ref):
    qi = pl.program_id(2)
    q = q_ref[0, 0]

    def block(ki, carry, masked):
        m, l, acc = carry
        off = pl.multiple_of(ki * ATT_TK, ATT_TK)
        kb = k_ref[0, 0, pl.ds(off, ATT_TK), :]
        vb = v_ref[0, 0, pl.ds(off, ATT_TK), :]
        s = _dot_nt(q, kb)
        if masked:
            s = jnp.where(pk_ref[0, :, pl.ds(off, ATT_TK)] <= pq_ref[0], s, NEG_BIG)
        m_new = jnp.maximum(m, jnp.max(s, axis=1, keepdims=True))
        alpha = jnp.exp2(m - m_new)
        p = jnp.exp2(s - m_new)
        l = alpha * l + jnp.sum(p, axis=1, keepdims=True)
        acc = alpha * acc + _dot(p.astype(BF16), vb)
        return m_new, l, acc

    init = (jnp.full((ATT_TQ, 1), NEG_BIG, F32), jnp.zeros((ATT_TQ, 1), F32), jnp.zeros((ATT_TQ, HEAD_PAD), F32))
    carry = lax.fori_loop(0, qi, lambda ki, c: block(ki, c, False), init)
    _, l, acc = block(qi, carry, True)
    o_ref[0, 0] = (acc / l).astype(BF16)


def _attention(q, k, v, pos_col, pos_row):
    nq = SEQ // ATT_TQ
    qspec = pl.BlockSpec((1, 1, ATT_TQ, HEAD_PAD), lambda b, h, i: (b, h, i, 0))
    kvspec = pl.BlockSpec((1, 1, SEQ, HEAD_PAD), lambda b, h, i: (b, h, 0, 0))
    return pl.pallas_call(
        _attn_kernel,
        grid=(BATCH, N_HEADS, nq),
        in_specs=[
            qspec, kvspec, kvspec,
            pl.BlockSpec((1, ATT_TQ, 1), lambda b, h, i: (b, i, 0)),
            pl.BlockSpec((1, 1, SEQ), lambda b, h, i: (b, 0, 0)),
        ],
        out_specs=qspec,
        out_shape=jax.ShapeDtypeStruct((BATCH, N_HEADS, SEQ, HEAD_PAD), BF16),
        compiler_params=_cparams(3),
        name="mla_attention",
    )(q, k, v, pos_col, pos_row)


MIX_TM = 512


def _gelu_tanh(x):
    return x * (0.5 * (1.0 + jnp.tanh(math.sqrt(2.0 / math.pi) * (x + 0.044715 * (x * x * x)))))


def _mix_kernel(x_ref, y_ref, o_ref, ga_ref, gb_ref, mod_ref, wglu_ref, bglu_ref, wa_ref, wb_ref, wo_ref, out_ref):
    y = _gelu_tanh(y_ref[...])
    gl = _dot(y.astype(BF16), wglu_ref[...]) + bglu_ref[...]
    ya = _dot((y * _sigmoid(gl)).astype(BF16), wa_ref[...])
    o = jnp.concatenate([o_ref[0, hd] for hd in range(N_HEADS)], axis=1)
    yb = _dot(o, wb_ref[...])
    merged = ga_ref[...] * ya + gb_ref[...] * yb
    g1 = mod_ref[0][:, 2 * D_MODEL:3 * D_MODEL]
    out_ref[...] = x_ref[...] + g1 * _dot(merged.astype(BF16), wo_ref[...])


def _mix(x, y_ssm, o, ga, gb, mod_l, wglu, bglu, wa, wb, wo):
    tm = MIX_TM
    nt = SEQ // tm
    const = lambda shape: pl.BlockSpec(shape, lambda i: (0,) * len(shape))
    tok = lambda width: pl.BlockSpec((tm, width), lambda i: (i, 0))
    return pl.pallas_call(
        _mix_kernel,
        grid=(TOKENS // tm,),
        in_specs=[
            tok(D_MODEL), tok(SSM_WIDTH),
            pl.BlockSpec((1, N_HEADS, tm, HEAD_PAD), lambda i: (i // nt, 0, i % nt, 0)),
            tok(D_MODEL), tok(D_MODEL),
            pl.BlockSpec((1, 1, N_MOD * D_MODEL), lambda i: (i // nt, 0, 0)),
            const((SSM_WIDTH, SSM_WIDTH)), const((1, SSM_WIDTH)),
            const((SSM_WIDTH, D_MODEL)), const((N_HEADS * HEAD_PAD, D_MODEL)), const((D_MODEL, D_MODEL)),
        ],
        out_specs=tok(D_MODEL),
        out_shape=jax.ShapeDtypeStruct((TOKENS, D_MODEL), F32),
        compiler_params=_cparams(1),
        name="branch_mix",
    )(x, y_ssm, o, ga, gb, mod_l, wglu, bglu, wa, wb, wo)


FFN_TM = 1024
FFN_CHUNKS = 2
FFN_TF = D_FF // FFN_CHUNKS


def _ffn_kernel(x_ref, mod_ref, g2_ref, wg_ref, wu_ref, wd_ref, fg_ref, out_ref, h_scr, acc_scr, *, final_norm):
    c = pl.program_id(1)

    @pl.when(c == 0)
    def _():
        x = x_ref[...]
        mod = mod_ref[0]
        sh = mod[:, 3 * D_MODEL:4 * D_MODEL]
        sc = mod[:, 4 * D_MODEL:5 * D_MODEL]
        xn = x * lax.rsqrt(jnp.mean(x * x, axis=-1, keepdims=True) + EPS) * g2_ref[...]
        h_scr[...] = (xn * (1.0 + sc) + sh).astype(BF16)

    h = h_scr[...]
    g = _dot(h, wg_ref[...])
    act = (g * _sigmoid(g)) * _dot(h, wu_ref[...])
    part = _dot(act.astype(BF16), wd_ref[...])

    @pl.when(c == 0)
    def _():
        acc_scr[...] = part

    @pl.when(c > 0)
    def _():
        acc_scr[...] += part

    @pl.when(c == FFN_CHUNKS - 1)
    def _():
        y = x_ref[...] + mod_ref[0][:, 5 * D_MODEL:6 * D_MODEL] * acc_scr[...]
        if final_norm:
            y = y * lax.rsqrt(jnp.mean(y * y, axis=-1, keepdims=True) + EPS) * fg_ref[...]
        out_ref[...] = y


def _ffn(x, mod_l, g2, wg, wu, wd, fg, final_norm):
    tm = FFN_TM
    nt = SEQ // tm
    return pl.pallas_call(
        functools.partial(_ffn_kernel, final_norm=final_norm),
        grid=(TOKENS // tm, FFN_CHUNKS),
        in_specs=[
            pl.BlockSpec((tm, D_MODEL), lambda i, c: (i, 0)),
            pl.BlockSpec((1, 1, N_MOD * D_MODEL), lambda i, c: (i // nt, 0, 0)),
            pl.BlockSpec((1, D_MODEL), lambda i, c: (0, 0)),
            pl.BlockSpec((D_MODEL, FFN_TF), lambda i, c: (0, c)),
            pl.BlockSpec((D_MODEL, FFN_TF), lambda i, c: (0, c)),
            pl.BlockSpec((FFN_TF, D_MODEL), lambda i, c: (c, 0)),
            pl.BlockSpec((1, D_MODEL), lambda i, c: (0, 0)),
        ],
        out_specs=pl.BlockSpec((tm, D_MODEL), lambda i, c: (i, 0)),
        out_shape=jax.ShapeDtypeStruct((TOKENS, D_MODEL), F32),
        scratch_shapes=[pltpu.VMEM((tm, D_MODEL), BF16), pltpu.VMEM((tm, D_MODEL), F32)],
        compiler_params=_cparams(2),
        name="swiglu_ffn",
    )(x, mod_l, g2, wg, wu, wd, fg)


def _pad_heads(w, dim):
    k = w.shape[0]
    w = w.reshape(k, N_HEADS, dim)
    return jnp.pad(w, ((0, 0), (0, 0), (0, HEAD_PAD - dim))).reshape(k, N_HEADS * HEAD_PAD)


def _swap_rope_halves(w_rope):
    half = QK_ROPE // 2
    return jnp.concatenate([w_rope[..., half:], w_rope[..., :half]], axis=-1)


def kernel(x, c, positions, w_ada, b_ada, norm1_g, w_in, ssm_a_re, ssm_a_im, ssm_log_dt, ssm_b_re, ssm_b_im,
           ssm_c_re, ssm_c_im, ssm_d, w_glu, b_glu, w_a_out, q_norm_g, w_uq, kv_norm_g, w_uk, w_uv, w_b_out,
           w_out, norm2_g, w_gate, w_up, w_down, final_g):
    inv_freq = ROPE_BASE ** (-jnp.arange(0, QK_ROPE, 2, dtype=F32) / QK_ROPE)
    half = QK_ROPE // 2
    freq_lanes = jnp.zeros((1, LANES), F32).at[0, QK_NOPE:QK_DIM].set(jnp.tile(inv_freq, 2))
    sign_lanes = (jnp.zeros((1, LANES), F32).at[0, QK_NOPE:QK_NOPE + half].set(-1.0)
                  .at[0, QK_NOPE + half:QK_DIM].set(1.0))
    cos_t, sin_t = _rope_tables(positions, freq_lanes, sign_lanes)
    pos_col = positions.reshape(BATCH, SEQ, 1)
    pos_row = positions.reshape(BATCH, 1, SEQ)

    mods = _adaln_mods(jnp.pad(c, ((0, 8 - BATCH), (0, 0))), w_ada, b_ada)
    m_mat, ws, wot, dec = _ssm_prepare(ssm_a_re, ssm_a_im, ssm_log_dt, ssm_b_re, ssm_b_im, ssm_c_re, ssm_c_im)
    d_skip = ssm_d.reshape(DEPTH, SSM_QBLOCKS, 1, LANES)

    xt = x.reshape(TOKENS, D_MODEL)
    for l in range(DEPTH):
        mod_l = mods[l, :BATCH].reshape(BATCH, 1, N_MOD * D_MODEL)
        wi = w_in[l]
        o_u, o_cq, o_ckv, o_kr, o_ga = (SSM_WIDTH, SSM_WIDTH + Q_LORA, SSM_WIDTH + Q_LORA + KV_LORA,
                                         SSM_WIDTH + Q_LORA + KV_LORA + QK_ROPE,
                                         SSM_WIDTH + Q_LORA + KV_LORA + QK_ROPE + D_MODEL)
        w_kr = wi[:, o_ckv:o_kr]
        rope_pad = ((0, 0), (QK_NOPE, HEAD_PAD - QK_DIM))
        w_all = jnp.concatenate([
            wi[:, :o_ckv],
            jnp.pad(w_kr, rope_pad), jnp.pad(_swap_rope_halves(w_kr), rope_pad),
            wi[:, o_kr:],
        ], axis=1).astype(BF16)
        uq = w_uq[l].reshape(Q_LORA, N_HEADS, QK_DIM)
        uq_swapped = jnp.concatenate([jnp.zeros_like(uq[..., :QK_NOPE]), _swap_rope_halves(uq[..., QK_NOPE:])], axis=-1)
        wq1 = _pad_heads(w_uq[l], QK_DIM).astype(BF16)
        wq2 = _pad_heads(uq_swapped.reshape(Q_LORA, N_HEADS * QK_DIM), QK_DIM).astype(BF16)
        wk = _pad_heads(w_uk[l], QK_NOPE).astype(BF16)
        wv = _pad_heads(w_uv[l], V_DIM).astype(BF16)

        u, q, k, v, ga, gb = _in_proj(xt, mod_l, norm1_g[l].reshape(1, D_MODEL), w_all,
                                      q_norm_g[l].reshape(1, Q_LORA), kv_norm_g[l].reshape(1, KV_LORA),
                                      wq1, wq2, wk, wv, cos_t, sin_t)
        y_ssm = _ssm_scan(u, m_mat, ws, wot, dec, d_skip, l)
        o = _attention(q, k, v, pos_col, pos_row)
        wb = jnp.pad(w_b_out[l].reshape(N_HEADS, V_DIM, D_MODEL),
                     ((0, 0), (0, HEAD_PAD - V_DIM), (0, 0))).reshape(N_HEADS * HEAD_PAD, D_MODEL).astype(BF16)
        xt = _mix(xt, y_ssm, o, ga, gb, mod_l, w_glu[l].astype(BF16), b_glu[l].reshape(1, SSM_WIDTH),
                  w_a_out[l].astype(BF16), wb, w_out[l].astype(BF16))
        xt = _ffn(xt, mod_l, norm2_g[l].reshape(1, D_MODEL), w_gate[l].astype(BF16), w_up[l].astype(BF16),
                  w_down[l].astype(BF16), final_g.reshape(1, D_MODEL), final_norm=(l == DEPTH - 1))
    return xt.reshape(BATCH, SEQ, D_MODEL)
```

```python
import functools
import math

import jax
import jax.numpy as jnp
import numpy as np
from jax import lax
from jax.experimental import pallas as pl
from jax.experimental.pallas import tpu as pltpu

D_MODEL = 1024
BATCH = 2
SEQ = 16384
DEPTH = 4
TOKENS = BATCH * SEQ

SSM_WIDTH = D_MODEL // 2
SSM_GROUP = 16
SSM_GROUPS = SSM_WIDTH // SSM_GROUP
SSM_STATE = 64
N_HEADS = 8
QK_NOPE = 64
QK_ROPE = 32
QK_DIM = QK_NOPE + QK_ROPE
V_DIM = 64
Q_LORA = 256
KV_LORA = 128
ROPE_BASE = 10000.0
D_FF = 2816
N_MOD = 6
EPS = 1e-6

LANES = 128
HEAD_PAD = LANES
SSM_CHUNK = 16
GROUPS_PER_TILE = LANES // SSM_GROUP
SSM_QBLOCKS = SSM_WIDTH // LANES
STATE_LANES = GROUPS_PER_TILE * SSM_STATE
CHUNK_LANES = SSM_CHUNK * LANES
VMEM_LIMIT = 56 * 1024 * 1024

F32 = jnp.float32
BF16 = jnp.bfloat16
NEG_BIG = -1e30


def _cparams(n_axes, vmem=VMEM_LIMIT):
    return pltpu.CompilerParams(dimension_semantics=("arbitrary",) * n_axes, vmem_limit_bytes=vmem)


def _sigmoid(x):
    return 1.0 / (1.0 + jnp.exp(-x))


def _split_bf16(a):
    hi = a.astype(BF16)
    lo = (a - hi.astype(F32)).astype(BF16)
    return hi, lo


def _dot(a, b):
    return jnp.dot(a, b, preferred_element_type=F32)


def _dot_nt(a, b):
    return lax.dot_general(a, b, (((1,), (1,)), ((), ())), preferred_element_type=F32)


def _dot3(a, b, nt=False):
    f = _dot_nt if nt else _dot
    ah, al = _split_bf16(a)
    bh, bl = _split_bf16(b)
    return f(ah, bh) + (f(al, bh) + f(ah, bl))


def _mod_kernel(c_ref, w_ref, b_ref, o_ref):
    c = c_ref[...]
    c_act = c * _sigmoid(c)
    o_ref[0] = _dot3(c_act, w_ref[0]) + b_ref[0]


def _adaln_mods(c_pad, w_ada, b_ada):
    ncol = N_MOD * D_MODEL
    blk = D_MODEL
    return pl.pallas_call(
        _mod_kernel,
        grid=(DEPTH, ncol // blk),
        in_specs=[
            pl.BlockSpec((8, D_MODEL), lambda l, j: (0, 0)),
            pl.BlockSpec((1, D_MODEL, blk), lambda l, j: (l, 0, j)),
            pl.BlockSpec((1, 1, blk), lambda l, j: (l, 0, j)),
        ],
        out_specs=pl.BlockSpec((1, 8, blk), lambda l, j: (l, 0, j)),
        out_shape=jax.ShapeDtypeStruct((DEPTH, 8, ncol), F32),
        compiler_params=_cparams(2),
        name="adaln_mod",
    )(c_pad, w_ada, b_ada.reshape(DEPTH, 1, ncol))


def _rope_kernel(pos_ref, freq_ref, sign_ref, c_ref, s_ref):
    ang = pos_ref[0].astype(F32) * freq_ref[...]
    c_ref[0] = jnp.cos(ang)
    s_ref[0] = jnp.sin(ang) * sign_ref[...]


def _rope_tables(positions, freq_lanes, sign_lanes):
    tm = 2048
    spec = pl.BlockSpec((1, tm, LANES), lambda b, i: (b, i, 0))
    vec = pl.BlockSpec((1, LANES), lambda b, i: (0, 0))
    return pl.pallas_call(
        _rope_kernel,
        grid=(BATCH, SEQ // tm),
        in_specs=[pl.BlockSpec((1, tm, 1), lambda b, i: (b, i, 0)), vec, vec],
        out_specs=[spec, spec],
        out_shape=[jax.ShapeDtypeStruct((BATCH, SEQ, LANES), F32)] * 2,
        compiler_params=_cparams(2),
        name="rope_tables",
    )(positions.reshape(BATCH, SEQ, 1), freq_lanes, sign_lanes)


def _ssm_prep_kernel(ar_ref, ai_ref, ldt_ref, bre_ref, bim_ref, cre_ref, cim_ref,
                     m_ref, ws_ref, wot_ref, dec_ref):
    ar = ar_ref[0, 0]
    ai = ai_ref[0, 0]
    dt = jnp.exp(ldt_ref[0, 0])
    mag = jnp.exp(ar * dt)
    ab_re = mag * jnp.cos(ai * dt)
    ab_im = mag * jnp.sin(ai * dt)
    den = ar * ar + ai * ai
    nr = ab_re - 1.0
    ni = ab_im
    coef_re = (nr * ar + ni * ai) / den
    coef_im = (ni * ar - nr * ai) / den
    b_re = bre_ref[0, 0]
    b_im = bim_ref[0, 0]
    bb_re = coef_re * b_re - coef_im * b_im
    bb_im = coef_re * b_im + coef_im * b_re
    c_re = cre_ref[0, 0]
    c_im = cim_ref[0, 0]

    pw = [(jnp.ones_like(ab_re), jnp.zeros_like(ab_im))]
    for _ in range(SSM_CHUNK):
        pr, pi = pw[-1]
        pw.append((pr * ab_re - pi * ab_im, pr * ab_im + pi * ab_re))

    bb_cat = jnp.concatenate([bb_re, -bb_im], axis=1)
    r0 = []
    for j in range(SSM_CHUNK + 1):
        pr, pi = pw[j]
        e_re = c_re * pr - c_im * pi
        e_im = c_re * pi + c_im * pr
        if j < SSM_CHUNK:
            r0.append(_dot3(bb_cat, jnp.concatenate([e_re, e_im], axis=1), nt=True).astype(BF16))
            s = SSM_CHUNK - 1 - j
            ws_ref[0, 0, s * LANES:(s + 1) * LANES, :] = jnp.concatenate(
                [bb_re * pr - bb_im * pi, bb_re * pi + bb_im * pr], axis=1).astype(BF16)
        if j >= 1:
            wot_ref[0, 0, (j - 1) * LANES:j * LANES, :] = jnp.concatenate([e_re, -e_im], axis=1).astype(BF16)
    zero = jnp.zeros((LANES, LANES), BF16)
    for s in range(SSM_CHUNK):
        for j in range(SSM_CHUNK):
            m_ref[0, 0, s * LANES:(s + 1) * LANES, j * LANES:(j + 1) * LANES] = r0[j - s] if j >= s else zero
    pr, pi = pw[SSM_CHUNK]
    dec_ref[0, 0] = jnp.concatenate([pr, pi], axis=1)


def _ssm_prepare(a_re, a_im, log_dt, b_re, b_im, c_re, c_im):
    nq, gt, p, m = SSM_QBLOCKS, GROUPS_PER_TILE, SSM_STATE, SSM_GROUP
    eye = jnp.eye(gt, dtype=F32)

    def lanes(v):
        return v.reshape(DEPTH, nq, 1, gt * p)

    def place(v):
        rows = v.shape[3]
        out = v[:, :, :, :, None, :] * eye[None, None, :, None, :, None]
        return out.reshape(DEPTH, nq, gt * rows, gt * p)

    ldt = jnp.broadcast_to(log_dt[:, :, None], (DEPTH, SSM_GROUPS, p))
    bt_re = place(jnp.swapaxes(b_re, 2, 3).reshape(DEPTH, nq, gt, m, p))
    bt_im = place(jnp.swapaxes(b_im, 2, 3).reshape(DEPTH, nq, gt, m, p))
    ct_re = place(c_re.reshape(DEPTH, nq, gt, m, p))
    ct_im = place(c_im.reshape(DEPTH, nq, gt, m, p))

    vec = pl.BlockSpec((1, 1, 1, STATE_LANES), lambda l, q: (l, q, 0, 0))
    mat = pl.BlockSpec((1, 1, LANES, STATE_LANES), lambda l, q: (l, q, 0, 0))
    return pl.pallas_call(
        _ssm_prep_kernel,
        grid=(DEPTH, nq),
        in_specs=[vec, vec, vec, mat, mat, mat, mat],
        out_specs=[
            pl.BlockSpec((1, 1, CHUNK_LANES, CHUNK_LANES), lambda l, q: (l, q, 0, 0)),
            pl.BlockSpec((1, 1, CHUNK_LANES, 2 * STATE_LANES), lambda l, q: (l, q, 0, 0)),
            pl.BlockSpec((1, 1, CHUNK_LANES, 2 * STATE_LANES), lambda l, q: (l, q, 0, 0)),
            pl.BlockSpec((1, 1, 1, 2 * STATE_LANES), lambda l, q: (l, q, 0, 0)),
        ],
        out_shape=[
            jax.ShapeDtypeStruct((DEPTH, nq, CHUNK_LANES, CHUNK_LANES), BF16),
            jax.ShapeDtypeStruct((DEPTH, nq, CHUNK_LANES, 2 * STATE_LANES), BF16),
            jax.ShapeDtypeStruct((DEPTH, nq, CHUNK_LANES, 2 * STATE_LANES), BF16),
            jax.ShapeDtypeStruct((DEPTH, nq, 1, 2 * STATE_LANES), F32),
        ],
        compiler_params=_cparams(2),
        name="ssm_prepare",
    )(lanes(a_re), lanes(a_im), lanes(ldt), bt_re, bt_im, ct_re, ct_im)


SSM_ROWS = 256
SSM_TOK = SSM_ROWS * SSM_CHUNK


def _ssm_kernel(u_ref, m_ref, ws_ref, wot_ref, dec_ref, d_ref, y_ref, s_scr, h_scr, carry_scr):
    r = pl.program_id(2)

    @pl.when(r == 0)
    def _():
        carry_scr[...] = jnp.zeros_like(carry_scr)

    x = jnp.concatenate([u_ref[pl.ds(s, SSM_ROWS, stride=SSM_CHUNK), :] for s in range(SSM_CHUNK)], axis=1)
    xb = x.astype(BF16)
    s_scr[...] = _dot(xb, ws_ref[0, 0])

    dec = dec_ref[0, 0]
    d_re = dec[:, :STATE_LANES]
    d_im = dec[:, STATE_LANES:]

    def body(i, h):
        base = pl.multiple_of(i * 8, 8)
        blk = s_scr[pl.ds(base, 8), :]
        rows = []
        for k in range(8):
            rows.append(h)
            h_re = h[:, :STATE_LANES]
            h_im = h[:, STATE_LANES:]
            inc = blk[k:k + 1, :]
            n_re = d_re * h_re - d_im * h_im + inc[:, :STATE_LANES]
            n_im = d_re * h_im + d_im * h_re + inc[:, STATE_LANES:]
            h = jnp.concatenate([n_re, n_im], axis=1)
        h_scr[pl.ds(base, 8), :] = jnp.concatenate(rows, axis=0)
        return h

    carry_scr[...] = lax.fori_loop(0, SSM_ROWS // 8, body, carry_scr[...])

    d_skip = jnp.concatenate([d_ref[0, 0]] * SSM_CHUNK, axis=1)
    y = _dot(xb, m_ref[0, 0]) + _dot_nt(h_scr[...].astype(BF16), wot_ref[0, 0]) + d_skip * x
    for j in range(SSM_CHUNK):
        y_ref[pl.ds(j, SSM_ROWS, stride=SSM_CHUNK), :] = y[:, j * LANES:(j + 1) * LANES]


def _ssm_scan(u, m_mat, ws, wot, dec, d_skip, layer):
    nr = SEQ // SSM_TOK
    tok = pl.BlockSpec((SSM_TOK, LANES), lambda q, b, r: (b * nr + r, q))
    return pl.pallas_call(
        _ssm_kernel,
        grid=(SSM_QBLOCKS, BATCH, nr),
        in_specs=[
            tok,
            pl.BlockSpec((1, 1, CHUNK_LANES, CHUNK_LANES), lambda q, b, r: (layer, q, 0, 0)),
            pl.BlockSpec((1, 1, CHUNK_LANES, 2 * STATE_LANES), lambda q, b, r: (layer, q, 0, 0)),
            pl.BlockSpec((1, 1, CHUNK_LANES, 2 * STATE_LANES), lambda q, b, r: (layer, q, 0, 0)),
            pl.BlockSpec((1, 1, 1, 2 * STATE_LANES), lambda q, b, r: (layer, q, 0, 0)),
            pl.BlockSpec((1, 1, 1, LANES), lambda q, b, r: (layer, q, 0, 0)),
        ],
        out_specs=tok,
        out_shape=jax.ShapeDtypeStruct((TOKENS, SSM_WIDTH), F32),
        scratch_shapes=[
            pltpu.VMEM((SSM_ROWS, 2 * STATE_LANES), F32),
            pltpu.VMEM((SSM_ROWS, 2 * STATE_LANES), F32),
            pltpu.VMEM((1, 2 * STATE_LANES), F32),
        ],
        compiler_params=_cparams(3),
        name="ssm_scan",
    )(u, m_mat, ws, wot, dec, d_skip)


IN_TM = 512
VT_ROWS = V_DIM + 16
_U0, _CQ0, _CKV0, _KA0, _KB0, _GA0, _GB0, _IN_END = 0, 512, 768, 896, 1024, 1152, 2176, 3200


def _in_kernel(x_ref, mod_ref, g1_ref, w_ref, qg_ref, kvg_ref, wq1_ref, wq2_ref, wk_ref, wvt_ref,
               c_ref, s_ref, u_ref, q_ref, k_ref, vt_ref, ga_ref, gb_ref, *, q_scale):
    x = x_ref[...]
    mod = mod_ref[0]
    sh = mod[:, 0:D_MODEL]
    sc = mod[:, D_MODEL:2 * D_MODEL]
    xn = x * lax.rsqrt(jnp.mean(x * x, axis=-1, keepdims=True) + EPS) * g1_ref[...]
    h = (xn * (1.0 + sc) + sh).astype(BF16)
    z = _dot(h, w_ref[...])

    u_ref[...] = z[:, _U0:_CQ0]
    ga_ref[...] = _sigmoid(z[:, _GA0:_GB0])
    gb_ref[...] = _sigmoid(z[:, _GB0:_IN_END])

    cos = c_ref[0]
    sin = s_ref[0]
    cq = z[:, _CQ0:_CKV0]
    cqn = (cq * lax.rsqrt(jnp.mean(cq * cq, axis=-1, keepdims=True) + EPS) * qg_ref[...]).astype(BF16)
    qa = _dot(cqn, wq1_ref[...])
    qb = _dot(cqn, wq2_ref[...])
    ckv = z[:, _CKV0:_KA0]
    ckvn = (ckv * lax.rsqrt(jnp.mean(ckv * ckv, axis=-1, keepdims=True) + EPS) * kvg_ref[...]).astype(BF16)
    kn = _dot(ckvn, wk_ref[...])
    vvt = _dot_nt(wvt_ref[...], ckvn)
    k_pe = z[:, _KA0:_KB0] * cos + z[:, _KB0:_GA0] * sin
    for hd in range(N_HEADS):
        sl = slice(hd * HEAD_PAD, (hd + 1) * HEAD_PAD)
        q_ref[0, hd] = ((qa[:, sl] * cos + qb[:, sl] * sin) * q_scale).astype(BF16)
        k_ref[0, hd] = (kn[:, sl] + k_pe).astype(BF16)
        vt_ref[0, hd, 0, :V_DIM, :] = vvt[hd * V_DIM:(hd + 1) * V_DIM, :].astype(BF16)
        vt_ref[0, hd, 0, V_DIM:, :] = jnp.ones((VT_ROWS - V_DIM, IN_TM), BF16)


def _in_proj(x, mod_l, g1, w_all, qg, kvg, wq1, wq2, wk, wvt, cos_t, sin_t):
    tm = IN_TM
    assert tm == ATT_TK
    nt = SEQ // tm
    const = lambda shape: pl.BlockSpec(shape, lambda i: (0,) * len(shape))
    tok = lambda width: pl.BlockSpec((tm, width), lambda i: (i, 0))
    head = pl.BlockSpec((1, N_HEADS, tm, HEAD_PAD), lambda i: (i // nt, 0, i % nt, 0))
    tab = pl.BlockSpec((1, tm, LANES), lambda i: (i // nt, i % nt, 0))
    hshape = jax.ShapeDtypeStruct((BATCH, N_HEADS, SEQ, HEAD_PAD), BF16)
    q_scale = QK_DIM ** -0.5 * math.log2(math.e)
    return pl.pallas_call(
        functools.partial(_in_kernel, q_scale=q_scale),
        grid=(TOKENS // tm,),
        in_specs=[
            tok(D_MODEL),
            pl.BlockSpec((1, 1, N_MOD * D_MODEL), lambda i: (i // nt, 0, 0)),
            const((1, D_MODEL)),
            const((D_MODEL, _IN_END)),
            const((1, Q_LORA)), const((1, KV_LORA)),
            const((Q_LORA, N_HEADS * HEAD_PAD)), const((Q_LORA, N_HEADS * HEAD_PAD)),
            const((KV_LORA, N_HEADS * HEAD_PAD)), const((N_HEADS * V_DIM, KV_LORA)),
            tab, tab,
        ],
        out_specs=[tok(SSM_WIDTH), head, head,
                   pl.BlockSpec((1, N_HEADS, 1, VT_ROWS, tm), lambda i: (i // nt, 0, i % nt, 0, 0)),
                   tok(D_MODEL), tok(D_MODEL)],
        out_shape=[
            jax.ShapeDtypeStruct((TOKENS, SSM_WIDTH), F32), hshape, hshape,
            jax.ShapeDtypeStruct((BATCH, N_HEADS, SEQ // tm, VT_ROWS, tm), BF16),
            jax.ShapeDtypeStruct((TOKENS, D_MODEL), F32), jax.ShapeDtypeStruct((TOKENS, D_MODEL), F32),
        ],
        compiler_params=_cparams(1),
        name="in_proj",
    )(x, mod_l, g1, w_all, qg, kvg, wq1, wq2, wk, wvt, cos_t, sin_t)


ATT_TQ = 2048
ATT_TK = 512
ATT_KB_PER_Q = ATT_TQ // ATT_TK


def _attn_kernel(q_ref, k_ref, vt_ref, pcol_ref, prow_ref, o_ref):
    qi = pl.program_id(2)

    def block(ki, carry, q_lo, key_pos):
        m, acc = carry
        off = pl.multiple_of(ki * ATT_TK, ATT_TK)
        kb = k_ref[0, 0, pl.ds(off, ATT_TK), :]
        vtb = vt_ref[0, 0, ki]
        s = _dot_nt(kb, q_ref[0, 0, q_lo:, :])
        if key_pos is not None:
            s = jnp.where(key_pos <= prow_ref[0, :, q_lo:], s, NEG_BIG)
        m_old = m[:, q_lo:]
        m_new = jnp.maximum(m_old, jnp.max(s, axis=0, keepdims=True))
        alpha = jnp.exp2(m_old - m_new)
        p = jnp.exp2(s - m_new).astype(BF16)
        acc_new = alpha * acc[:, q_lo:] + _dot(vtb, p)
        if q_lo:
            m_new = jnp.concatenate([m[:, :q_lo], m_new], axis=1)
            acc_new = jnp.concatenate([acc[:, :q_lo], acc_new], axis=1)
        return m_new, acc_new

    def visible_blocks(i, carry):
        for j in range(ATT_KB_PER_Q):
            carry = block(ATT_KB_PER_Q * i + j, carry, 0, None)
        return carry

    init = (jnp.full((1, ATT_TQ), NEG_BIG, F32), jnp.zeros((VT_ROWS, ATT_TQ), F32))
    carry = lax.fori_loop(0, qi, visible_blocks, init)
    for j in range(ATT_KB_PER_Q):
        carry = block(ATT_KB_PER_Q * qi + j, carry, j * ATT_TK, pcol_ref[0, j * ATT_TK:(j + 1) * ATT_TK, :])
    _, acc = carry
    o_ref[0, 0] = (acc[:V_DIM] / acc[V_DIM:V_DIM + 1]).astype(BF16)


def _attention(q, k, vt, pos_col, pos_row):
    nq = SEQ // ATT_TQ
    return pl.pallas_call(
        _attn_kernel,
        grid=(BATCH, N_HEADS, nq),
        in_specs=[
            pl.BlockSpec((1, 1, ATT_TQ, HEAD_PAD), lambda b, h, i: (b, h, i, 0)),
            pl.BlockSpec((1, 1, SEQ, HEAD_PAD), lambda b, h, i: (b, h, 0, 0)),
            pl.BlockSpec((1, 1, SEQ // ATT_TK, VT_ROWS, ATT_TK), lambda b, h, i: (b, h, 0, 0, 0)),
            pl.BlockSpec((1, ATT_TQ, 1), lambda b, h, i: (b, i, 0)),
            pl.BlockSpec((1, 1, ATT_TQ), lambda b, h, i: (b, 0, i)),
        ],
        out_specs=pl.BlockSpec((1, 1, V_DIM, ATT_TQ), lambda b, h, i: (b, h, 0, i)),
        out_shape=jax.ShapeDtypeStruct((BATCH, N_HEADS, V_DIM, SEQ), BF16),
        compiler_params=_cparams(3),
        name="mla_attention",
    )(q, k, vt, pos_col, pos_row)


MIX_TM = 512


def _gelu_tanh(x):
    return x * (0.5 * (1.0 + jnp.tanh(math.sqrt(2.0 / math.pi) * (x + 0.044715 * (x * x * x)))))


def _mix_kernel(x_ref, y_ref, o_ref, ga_ref, gb_ref, mod_ref, wglu_ref, bglu_ref, wa_ref, wb_ref, wo_ref, out_ref):
    y = _gelu_tanh(y_ref[...])
    gl = _dot(y.astype(BF16), wglu_ref[...]) + bglu_ref[...]
    ya = _dot((y * _sigmoid(gl)).astype(BF16), wa_ref[...])
    ot = jnp.concatenate([o_ref[0, hd] for hd in range(N_HEADS)], axis=0)
    yb = lax.dot_general(ot, wb_ref[...], (((0,), (0,)), ((), ())), preferred_element_type=F32)
    merged = ga_ref[...] * ya + gb_ref[...] * yb
    g1 = mod_ref[0][:, 2 * D_MODEL:3 * D_MODEL]
    out_ref[...] = x_ref[...] + g1 * _dot(merged.astype(BF16), wo_ref[...])


def _mix(x, y_ssm, o, ga, gb, mod_l, wglu, bglu, wa, wb, wo):
    tm = MIX_TM
    nt = SEQ // tm
    const = lambda shape: pl.BlockSpec(shape, lambda i: (0,) * len(shape))
    tok = lambda width: pl.BlockSpec((tm, width), lambda i: (i, 0))
    return pl.pallas_call(
        _mix_kernel,
        grid=(TOKENS // tm,),
        in_specs=[
            tok(D_MODEL), tok(SSM_WIDTH),
            pl.BlockSpec((1, N_HEADS, V_DIM, tm), lambda i: (i // nt, 0, 0, i % nt)),
            tok(D_MODEL), tok(D_MODEL),
            pl.BlockSpec((1, 1, N_MOD * D_MODEL), lambda i: (i // nt, 0, 0)),
            const((SSM_WIDTH, SSM_WIDTH)), const((1, SSM_WIDTH)),
            const((SSM_WIDTH, D_MODEL)), const((N_HEADS * V_DIM, D_MODEL)), const((D_MODEL, D_MODEL)),
        ],
        out_specs=tok(D_MODEL),
        out_shape=jax.ShapeDtypeStruct((TOKENS, D_MODEL), F32),
        compiler_params=_cparams(1),
        name="branch_mix",
    )(x, y_ssm, o, ga, gb, mod_l, wglu, bglu, wa, wb, wo)


FFN_TM = 1024
FFN_CHUNKS = 2
FFN_TF = D_FF // FFN_CHUNKS


def _ffn_kernel(x_ref, mod_ref, g2_ref, wg_ref, wu_ref, wd_ref, fg_ref, out_ref, h_scr, acc_scr, *, final_norm):
    c = pl.program_id(1)

    @pl.when(c == 0)
    def _():
        x = x_ref[...]
        mod = mod_ref[0]
        sh = mod[:, 3 * D_MODEL:4 * D_MODEL]
        sc = mod[:, 4 * D_MODEL:5 * D_MODEL]
        xn = x * lax.rsqrt(jnp.mean(x * x, axis=-1, keepdims=True) + EPS) * g2_ref[...]
        h_scr[...] = (xn * (1.0 + sc) + sh).astype(BF16)

    h = h_scr[...]
    g = _dot(h, wg_ref[...])
    act = (g * _sigmoid(g)) * _dot(h, wu_ref[...])
    part = _dot(act.astype(BF16), wd_ref[...])

    @pl.when(c == 0)
    def _():
        acc_scr[...] = part

    @pl.when(c > 0)
    def _():
        acc_scr[...] += part

    @pl.when(c == FFN_CHUNKS - 1)
    def _():
        y = x_ref[...] + mod_ref[0][:, 5 * D_MODEL:6 * D_MODEL] * acc_scr[...]
        if final_norm:
            y = y * lax.rsqrt(jnp.mean(y * y, axis=-1, keepdims=True) + EPS) * fg_ref[...]
        out_ref[...] = y


def _ffn(x, mod_l, g2, wg, wu, wd, fg, final_norm):
    tm = FFN_TM
    nt = SEQ // tm
    return pl.pallas_call(
        functools.partial(_ffn_kernel, final_norm=final_norm),
        grid=(TOKENS // tm, FFN_CHUNKS),
        in_specs=[
            pl.BlockSpec((tm, D_MODEL), lambda i, c: (i, 0)),
            pl.BlockSpec((1, 1, N_MOD * D_MODEL), lambda i, c: (i // nt, 0, 0)),
            pl.BlockSpec((1, D_MODEL), lambda i, c: (0, 0)),
            pl.BlockSpec((D_MODEL, FFN_TF), lambda i, c: (0, c)),
            pl.BlockSpec((D_MODEL, FFN_TF), lambda i, c: (0, c)),
            pl.BlockSpec((FFN_TF, D_MODEL), lambda i, c: (c, 0)),
            pl.BlockSpec((1, D_MODEL), lambda i, c: (0, 0)),
        ],
        out_specs=pl.BlockSpec((tm, D_MODEL), lambda i, c: (i, 0)),
        out_shape=jax.ShapeDtypeStruct((TOKENS, D_MODEL), F32),
        scratch_shapes=[pltpu.VMEM((tm, D_MODEL), BF16), pltpu.VMEM((tm, D_MODEL), F32)],
        compiler_params=_cparams(2),
        name="swiglu_ffn",
    )(x, mod_l, g2, wg, wu, wd, fg)


def _pad_heads(w, dim):
    k = w.shape[0]
    w = w.reshape(k, N_HEADS, dim)
    return jnp.pad(w, ((0, 0), (0, 0), (0, HEAD_PAD - dim))).reshape(k, N_HEADS * HEAD_PAD)


def _swap_rope_halves(w_rope):
    half = QK_ROPE // 2
    return jnp.concatenate([w_rope[..., half:], w_rope[..., :half]], axis=-1)


def kernel(x, c, positions, w_ada, b_ada, norm1_g, w_in, ssm_a_re, ssm_a_im, ssm_log_dt, ssm_b_re, ssm_b_im,
           ssm_c_re, ssm_c_im, ssm_d, w_glu, b_glu, w_a_out, q_norm_g, w_uq, kv_norm_g, w_uk, w_uv, w_b_out,
           w_out, norm2_g, w_gate, w_up, w_down, final_g):
    inv_freq = ROPE_BASE ** (-jnp.arange(0, QK_ROPE, 2, dtype=F32) / QK_ROPE)
    half = QK_ROPE // 2
    freq_lanes = jnp.zeros((1, LANES), F32).at[0, QK_NOPE:QK_DIM].set(jnp.tile(inv_freq, 2))
    sign_lanes = (jnp.zeros((1, LANES), F32).at[0, QK_NOPE:QK_NOPE + half].set(-1.0)
                  .at[0, QK_NOPE + half:QK_DIM].set(1.0))
    cos_t, sin_t = _rope_tables(positions, freq_lanes, sign_lanes)
    pos_col = positions.reshape(BATCH, SEQ, 1)
    pos_row = positions.reshape(BATCH, 1, SEQ)

    mods = _adaln_mods(jnp.pad(c, ((0, 8 - BATCH), (0, 0))), w_ada, b_ada)
    m_mat, ws, wot, dec = _ssm_prepare(ssm_a_re, ssm_a_im, ssm_log_dt, ssm_b_re, ssm_b_im, ssm_c_re, ssm_c_im)
    d_skip = ssm_d.reshape(DEPTH, SSM_QBLOCKS, 1, LANES)

    xt = x.reshape(TOKENS, D_MODEL)
    for l in range(DEPTH):
        mod_l = mods[l, :BATCH].reshape(BATCH, 1, N_MOD * D_MODEL)
        wi = w_in[l]
        o_u, o_cq, o_ckv, o_kr, o_ga = (SSM_WIDTH, SSM_WIDTH + Q_LORA, SSM_WIDTH + Q_LORA + KV_LORA,
                                         SSM_WIDTH + Q_LORA + KV_LORA + QK_ROPE,
                                         SSM_WIDTH + Q_LORA + KV_LORA + QK_ROPE + D_MODEL)
        w_kr = wi[:, o_ckv:o_kr]
        rope_pad = ((0, 0), (QK_NOPE, HEAD_PAD - QK_DIM))
        w_all = jnp.concatenate([
            wi[:, :o_ckv],
            jnp.pad(w_kr, rope_pad), jnp.pad(_swap_rope_halves(w_kr), rope_pad),
            wi[:, o_kr:],
        ], axis=1).astype(BF16)
        uq = w_uq[l].reshape(Q_LORA, N_HEADS, QK_DIM)
        uq_swapped = jnp.concatenate([jnp.zeros_like(uq[..., :QK_NOPE]), _swap_rope_halves(uq[..., QK_NOPE:])], axis=-1)
        wq1 = _pad_heads(w_uq[l], QK_DIM).astype(BF16)
        wq2 = _pad_heads(uq_swapped.reshape(Q_LORA, N_HEADS * QK_DIM), QK_DIM).astype(BF16)
        wk = _pad_heads(w_uk[l], QK_NOPE).astype(BF16)
        wvt = w_uv[l].T.astype(BF16)

        u, q, k, vt, ga, gb = _in_proj(xt, mod_l, norm1_g[l].reshape(1, D_MODEL), w_all,
                                      q_norm_g[l].reshape(1, Q_LORA), kv_norm_g[l].reshape(1, KV_LORA),
                                      wq1, wq2, wk, wvt, cos_t, sin_t)
        y_ssm = _ssm_scan(u, m_mat, ws, wot, dec, d_skip, l)
        o = _attention(q, k, vt, pos_col, pos_row)
        xt = _mix(xt, y_ssm, o, ga, gb, mod_l, w_glu[l].astype(BF16), b_glu[l].reshape(1, SSM_WIDTH),
                  w_a_out[l].astype(BF16), w_b_out[l].astype(BF16), w_out[l].astype(BF16))
        xt = _ffn(xt, mod_l, norm2_g[l].reshape(1, D_MODEL), w_gate[l].astype(BF16), w_up[l].astype(BF16),
                  w_down[l].astype(BF16), final_g.reshape(1, D_MODEL), final_norm=(l == DEPTH - 1))
    return xt.reshape(BATCH, SEQ, D_MODEL)
```

```python
import functools
import math

import jax
import jax.numpy as jnp
import numpy as np
from jax import lax
from jax.experimental import pallas as pl
from jax.experimental.pallas import tpu as pltpu

D_MODEL = 1024
BATCH = 2
SEQ = 16384
DEPTH = 4
TOKENS = BATCH * SEQ

SSM_WIDTH = D_MODEL // 2
SSM_GROUP = 16
SSM_GROUPS = SSM_WIDTH // SSM_GROUP
SSM_STATE = 64
N_HEADS = 8
QK_NOPE = 64
QK_ROPE = 32
QK_DIM = QK_NOPE + QK_ROPE
V_DIM = 64
Q_LORA = 256
KV_LORA = 128
ROPE_BASE = 10000.0
D_FF = 2816
N_MOD = 6
EPS = 1e-6

LANES = 128
HEAD_PAD = LANES
SSM_CHUNK = 16
GROUPS_PER_TILE = LANES // SSM_GROUP
SSM_QBLOCKS = SSM_WIDTH // LANES
STATE_LANES = GROUPS_PER_TILE * SSM_STATE
CHUNK_LANES = SSM_CHUNK * LANES
VMEM_LIMIT = 56 * 1024 * 1024

F32 = jnp.float32
BF16 = jnp.bfloat16
NEG_BIG = -1e30


def _cparams(n_axes, vmem=VMEM_LIMIT, flags=None):
    return pltpu.CompilerParams(dimension_semantics=("arbitrary",) * n_axes, vmem_limit_bytes=vmem, flags=flags)


def _resident(shape):
    return pl.BlockSpec(shape, lambda *_: (0,) * len(shape), pipeline_mode=pl.Buffered(1))


def _sigmoid(x):
    return 1.0 / (1.0 + jnp.exp(-x))


def _split_bf16(a):
    hi = a.astype(BF16)
    lo = (a - hi.astype(F32)).astype(BF16)
    return hi, lo


def _dot(a, b):
    return jnp.dot(a, b, preferred_element_type=F32)


def _dot_nt(a, b):
    return lax.dot_general(a, b, (((1,), (1,)), ((), ())), preferred_element_type=F32)


def _dot3(a, b, nt=False):
    f = _dot_nt if nt else _dot
    ah, al = _split_bf16(a)
    bh, bl = _split_bf16(b)
    return f(ah, bh) + (f(al, bh) + f(ah, bl))


def _mod_kernel(c_ref, w_ref, b_ref, o_ref):
    c = c_ref[...]
    c_act = c * _sigmoid(c)
    o_ref[0] = _dot3(c_act, w_ref[0]) + b_ref[0]


def _adaln_mods(c_pad, w_ada, b_ada):
    ncol = N_MOD * D_MODEL
    blk = D_MODEL
    return pl.pallas_call(
        _mod_kernel,
        grid=(DEPTH, ncol // blk),
        in_specs=[
            pl.BlockSpec((8, D_MODEL), lambda l, j: (0, 0)),
            pl.BlockSpec((1, D_MODEL, blk), lambda l, j: (l, 0, j)),
            pl.BlockSpec((1, 1, blk), lambda l, j: (l, 0, j)),
        ],
        out_specs=pl.BlockSpec((1, 8, blk), lambda l, j: (l, 0, j)),
        out_shape=jax.ShapeDtypeStruct((DEPTH, 8, ncol), F32),
        compiler_params=_cparams(2),
        name="adaln_mod",
    )(c_pad, w_ada, b_ada.reshape(DEPTH, 1, ncol))


def _rope_kernel(pos_ref, freq_ref, sign_ref, c_ref, s_ref):
    ang = pos_ref[0].astype(F32) * freq_ref[...]
    c_ref[0] = jnp.cos(ang)
    s_ref[0] = jnp.sin(ang) * sign_ref[...]


def _rope_tables(positions, freq_lanes, sign_lanes):
    tm = 2048
    spec = pl.BlockSpec((1, tm, LANES), lambda b, i: (b, i, 0))
    vec = pl.BlockSpec((1, LANES), lambda b, i: (0, 0))
    return pl.pallas_call(
        _rope_kernel,
        grid=(BATCH, SEQ // tm),
        in_specs=[pl.BlockSpec((1, tm, 1), lambda b, i: (b, i, 0)), vec, vec],
        out_specs=[spec, spec],
        out_shape=[jax.ShapeDtypeStruct((BATCH, SEQ, LANES), F32)] * 2,
        compiler_params=_cparams(2),
        name="rope_tables",
    )(positions.reshape(BATCH, SEQ, 1), freq_lanes, sign_lanes)


def _ssm_prep_kernel(ar_ref, ai_ref, ldt_ref, bre_ref, bim_ref, cre_ref, cim_ref,
                     m_ref, ws_ref, wot_ref, dec_ref):
    ar = ar_ref[0, 0]
    ai = ai_ref[0, 0]
    dt = jnp.exp(ldt_ref[0, 0])
    mag = jnp.exp(ar * dt)
    ab_re = mag * jnp.cos(ai * dt)
    ab_im = mag * jnp.sin(ai * dt)
    den = ar * ar + ai * ai
    nr = ab_re - 1.0
    ni = ab_im
    coef_re = (nr * ar + ni * ai) / den
    coef_im = (ni * ar - nr * ai) / den
    b_re = bre_ref[0, 0]
    b_im = bim_ref[0, 0]
    bb_re = coef_re * b_re - coef_im * b_im
    bb_im = coef_re * b_im + coef_im * b_re
    c_re = cre_ref[0, 0]
    c_im = cim_ref[0, 0]

    pw = [(jnp.ones_like(ab_re), jnp.zeros_like(ab_im))]
    for _ in range(SSM_CHUNK):
        pr, pi = pw[-1]
        pw.append((pr * ab_re - pi * ab_im, pr * ab_im + pi * ab_re))

    bb_cat = jnp.concatenate([bb_re, -bb_im], axis=1)
    r0 = []
    for j in range(SSM_CHUNK + 1):
        pr, pi = pw[j]
        e_re = c_re * pr - c_im * pi
        e_im = c_re * pi + c_im * pr
        if j < SSM_CHUNK:
            r0.append(_dot3(bb_cat, jnp.concatenate([e_re, e_im], axis=1), nt=True).astype(BF16))
            s = SSM_CHUNK - 1 - j
            ws_ref[0, 0, s * LANES:(s + 1) * LANES, :] = jnp.concatenate(
                [bb_re * pr - bb_im * pi, bb_re * pi + bb_im * pr], axis=1).astype(BF16)
        if j >= 1:
            wot_ref[0, 0, (j - 1) * LANES:j * LANES, :] = jnp.concatenate([e_re, -e_im], axis=1).astype(BF16)
    zero = jnp.zeros((LANES, LANES), BF16)
    for s in range(SSM_CHUNK):
        for j in range(SSM_CHUNK):
            m_ref[0, 0, s * LANES:(s + 1) * LANES, j * LANES:(j + 1) * LANES] = r0[j - s] if j >= s else zero
    pr, pi = pw[SSM_CHUNK]
    dec_ref[0, 0] = jnp.concatenate([pr, pi], axis=1)


def _ssm_prepare(a_re, a_im, log_dt, b_re, b_im, c_re, c_im):
    nq, gt, p, m = SSM_QBLOCKS, GROUPS_PER_TILE, SSM_STATE, SSM_GROUP
    eye = jnp.eye(gt, dtype=F32)

    def lanes(v):
        return v.reshape(DEPTH, nq, 1, gt * p)

    def place(v):
        rows = v.shape[3]
        out = v[:, :, :, :, None, :] * eye[None, None, :, None, :, None]
        return out.reshape(DEPTH, nq, gt * rows, gt * p)

    ldt = jnp.broadcast_to(log_dt[:, :, None], (DEPTH, SSM_GROUPS, p))
    bt_re = place(jnp.swapaxes(b_re, 2, 3).reshape(DEPTH, nq, gt, m, p))
    bt_im = place(jnp.swapaxes(b_im, 2, 3).reshape(DEPTH, nq, gt, m, p))
    ct_re = place(c_re.reshape(DEPTH, nq, gt, m, p))
    ct_im = place(c_im.reshape(DEPTH, nq, gt, m, p))

    vec = pl.BlockSpec((1, 1, 1, STATE_LANES), lambda l, q: (l, q, 0, 0))
    mat = pl.BlockSpec((1, 1, LANES, STATE_LANES), lambda l, q: (l, q, 0, 0))
    return pl.pallas_call(
        _ssm_prep_kernel,
        grid=(DEPTH, nq),
        in_specs=[vec, vec, vec, mat, mat, mat, mat],
        out_specs=[
            pl.BlockSpec((1, 1, CHUNK_LANES, CHUNK_LANES), lambda l, q: (l, q, 0, 0)),
            pl.BlockSpec((1, 1, CHUNK_LANES, 2 * STATE_LANES), lambda l, q: (l, q, 0, 0)),
            pl.BlockSpec((1, 1, CHUNK_LANES, 2 * STATE_LANES), lambda l, q: (l, q, 0, 0)),
            pl.BlockSpec((1, 1, 1, 2 * STATE_LANES), lambda l, q: (l, q, 0, 0)),
        ],
        out_shape=[
            jax.ShapeDtypeStruct((DEPTH, nq, CHUNK_LANES, CHUNK_LANES), BF16),
            jax.ShapeDtypeStruct((DEPTH, nq, CHUNK_LANES, 2 * STATE_LANES), BF16),
            jax.ShapeDtypeStruct((DEPTH, nq, CHUNK_LANES, 2 * STATE_LANES), BF16),
            jax.ShapeDtypeStruct((DEPTH, nq, 1, 2 * STATE_LANES), F32),
        ],
        compiler_params=_cparams(2),
        name="ssm_prepare",
    )(lanes(a_re), lanes(a_im), lanes(ldt), bt_re, bt_im, ct_re, ct_im)


SSM_ROWS = 256
SSM_TOK = SSM_ROWS * SSM_CHUNK


def _ssm_kernel(u_ref, m_ref, ws_ref, wot_ref, dec_ref, d_ref, y_ref, s_scr, h_scr, carry_scr):
    r = pl.program_id(2)

    @pl.when(r == 0)
    def _():
        carry_scr[...] = jnp.zeros_like(carry_scr)

    x = jnp.concatenate([u_ref[pl.ds(s, SSM_ROWS, stride=SSM_CHUNK), :] for s in range(SSM_CHUNK)], axis=1)
    xb = x.astype(BF16)
    s_scr[...] = _dot(xb, ws_ref[0, 0])

    dec = dec_ref[0, 0]
    d_re = dec[:, :STATE_LANES]
    d_im = dec[:, STATE_LANES:]

    def body(i, h):
        base = pl.multiple_of(i * 8, 8)
        blk = s_scr[pl.ds(base, 8), :]
        rows = []
        for k in range(8):
            rows.append(h)
            h_re = h[:, :STATE_LANES]
            h_im = h[:, STATE_LANES:]
            inc = blk[k:k + 1, :]
            n_re = d_re * h_re - d_im * h_im + inc[:, :STATE_LANES]
            n_im = d_re * h_im + d_im * h_re + inc[:, STATE_LANES:]
            h = jnp.concatenate([n_re, n_im], axis=1)
        h_scr[pl.ds(base, 8), :] = jnp.concatenate(rows, axis=0)
        return h

    carry_scr[...] = lax.fori_loop(0, SSM_ROWS // 8, body, carry_scr[...])

    d_skip = jnp.concatenate([d_ref[0, 0]] * SSM_CHUNK, axis=1)
    y = _dot(xb, m_ref[0, 0]) + _dot_nt(h_scr[...].astype(BF16), wot_ref[0, 0]) + d_skip * x
    for j in range(SSM_CHUNK):
        y_ref[pl.ds(j, SSM_ROWS, stride=SSM_CHUNK), :] = y[:, j * LANES:(j + 1) * LANES]


def _ssm_scan(u, m_mat, ws, wot, dec, d_skip, layer):
    nr = SEQ // SSM_TOK
    tok = pl.BlockSpec((SSM_TOK, LANES), lambda q, b, r: (b * nr + r, q))
    return pl.pallas_call(
        _ssm_kernel,
        grid=(SSM_QBLOCKS, BATCH, nr),
        in_specs=[
            tok,
            pl.BlockSpec((1, 1, CHUNK_LANES, CHUNK_LANES), lambda q, b, r: (layer, q, 0, 0)),
            pl.BlockSpec((1, 1, CHUNK_LANES, 2 * STATE_LANES), lambda q, b, r: (layer, q, 0, 0)),
            pl.BlockSpec((1, 1, CHUNK_LANES, 2 * STATE_LANES), lambda q, b, r: (layer, q, 0, 0)),
            pl.BlockSpec((1, 1, 1, 2 * STATE_LANES), lambda q, b, r: (layer, q, 0, 0)),
            pl.BlockSpec((1, 1, 1, LANES), lambda q, b, r: (layer, q, 0, 0)),
        ],
        out_specs=tok,
        out_shape=jax.ShapeDtypeStruct((TOKENS, SSM_WIDTH), F32),
        scratch_shapes=[
            pltpu.VMEM((SSM_ROWS, 2 * STATE_LANES), F32),
            pltpu.VMEM((SSM_ROWS, 2 * STATE_LANES), F32),
            pltpu.VMEM((1, 2 * STATE_LANES), F32),
        ],
        compiler_params=_cparams(3),
        name="ssm_scan",
    )(u, m_mat, ws, wot, dec, d_skip)


IN_TM = 512
VT_ROWS = V_DIM + 16
_U0, _CQ0, _CKV0, _KA0, _KB0, _GA0, _GB0, _IN_END = 0, 512, 768, 896, 1024, 1152, 2176, 3200


def _in_kernel(x_ref, mod_ref, g1_ref, w_ref, qg_ref, kvg_ref, wq1_ref, wq2_ref, wk_ref, wvt_ref,
               c_ref, s_ref, u_ref, q_ref, k_ref, vt_ref, ga_ref, gb_ref, *, q_scale):
    x = x_ref[...]
    mod = mod_ref[0]
    sh = mod[:, 0:D_MODEL]
    sc = mod[:, D_MODEL:2 * D_MODEL]
    xn = x * lax.rsqrt(jnp.mean(x * x, axis=-1, keepdims=True) + EPS) * g1_ref[...]
    h = (xn * (1.0 + sc) + sh).astype(BF16)
    z = _dot(h, w_ref[...])

    u_ref[...] = z[:, _U0:_CQ0]
    ga_ref[...] = _sigmoid(z[:, _GA0:_GB0]).astype(BF16)
    gb_ref[...] = _sigmoid(z[:, _GB0:_IN_END]).astype(BF16)

    cos = c_ref[0]
    sin = s_ref[0]
    cq = z[:, _CQ0:_CKV0]
    cqn = (cq * lax.rsqrt(jnp.mean(cq * cq, axis=-1, keepdims=True) + EPS) * qg_ref[...]).astype(BF16)
    qa = _dot(cqn, wq1_ref[...])
    qb = _dot(cqn, wq2_ref[...])
    ckv = z[:, _CKV0:_KA0]
    ckvn = (ckv * lax.rsqrt(jnp.mean(ckv * ckv, axis=-1, keepdims=True) + EPS) * kvg_ref[...]).astype(BF16)
    kn = _dot(ckvn, wk_ref[...])
    vvt = _dot_nt(wvt_ref[...], ckvn)
    k_pe = z[:, _KA0:_KB0] * cos + z[:, _KB0:_GA0] * sin
    for hd in range(N_HEADS):
        sl = slice(hd * HEAD_PAD, (hd + 1) * HEAD_PAD)
        q_ref[0, hd] = ((qa[:, sl] * cos + qb[:, sl] * sin) * q_scale).astype(BF16)
        k_ref[0, hd] = (kn[:, sl] + k_pe).astype(BF16)
        vt_ref[0, hd, 0, :V_DIM, :] = vvt[hd * V_DIM:(hd + 1) * V_DIM, :].astype(BF16)
        vt_ref[0, hd, 0, V_DIM:, :] = jnp.ones((VT_ROWS - V_DIM, IN_TM), BF16)


def _in_proj(x, mod_l, g1, w_all, qg, kvg, wq1, wq2, wk, wvt, cos_t, sin_t):
    tm = IN_TM
    assert ATT_TK % tm == 0
    nt = SEQ // tm
    const = _resident
    tok = lambda width: pl.BlockSpec((tm, width), lambda i: (i, 0))
    head = pl.BlockSpec((1, N_HEADS, tm, HEAD_PAD), lambda i: (i // nt, 0, i % nt, 0))
    tab = pl.BlockSpec((1, tm, LANES), lambda i: (i // nt, i % nt, 0))
    hshape = jax.ShapeDtypeStruct((BATCH, N_HEADS, SEQ, HEAD_PAD), BF16)
    q_scale = QK_DIM ** -0.5 * math.log2(math.e)
    return pl.pallas_call(
        functools.partial(_in_kernel, q_scale=q_scale),
        grid=(TOKENS // tm,),
        in_specs=[
            tok(D_MODEL),
            pl.BlockSpec((1, 1, N_MOD * D_MODEL), lambda i: (i // nt, 0, 0)),
            const((1, D_MODEL)),
            const((D_MODEL, _IN_END)),
            const((1, Q_LORA)), const((1, KV_LORA)),
            const((Q_LORA, N_HEADS * HEAD_PAD)), const((Q_LORA, N_HEADS * HEAD_PAD)),
            const((KV_LORA, N_HEADS * HEAD_PAD)), const((N_HEADS * V_DIM, KV_LORA)),
            tab, tab,
        ],
        out_specs=[tok(SSM_WIDTH), head, head,
                   pl.BlockSpec((1, N_HEADS, 1, VT_ROWS, tm), lambda i: (i // nt, 0, i % nt, 0, 0)),
                   tok(D_MODEL), tok(D_MODEL)],
        out_shape=[
            jax.ShapeDtypeStruct((TOKENS, SSM_WIDTH), F32), hshape, hshape,
            jax.ShapeDtypeStruct((BATCH, N_HEADS, SEQ // tm, VT_ROWS, tm), BF16),
            jax.ShapeDtypeStruct((TOKENS, D_MODEL), BF16), jax.ShapeDtypeStruct((TOKENS, D_MODEL), BF16),
        ],
        compiler_params=_cparams(1),
        name="in_proj",
    )(x, mod_l, g1, w_all, qg, kvg, wq1, wq2, wk, wvt, cos_t, sin_t)


ATT_TQ = 2048
ATT_TK = 512
ATT_KB_PER_Q = ATT_TQ // ATT_TK
ATT_CW = 512
ATT_NCT = ATT_TQ // ATT_CW


def _attn_kernel(q_ref, k_ref, vt_ref, pcol_ref, prow_ref, o_ref):
    qi = pl.program_id(2)

    def keys(ki):
        return k_ref[0, 0, pl.ds(pl.multiple_of(ki * ATT_TK, ATT_TK), ATT_TK), :]

    def values_t(ki):
        nchunk = ATT_TK // IN_TM
        return jnp.concatenate([vt_ref[0, 0, ki * nchunk + c] for c in range(nchunk)], axis=1)

    def soft(s, m_old):
        m_new = jnp.maximum(m_old, jnp.max(s, axis=0, keepdims=True))
        return m_new, jnp.exp2(m_old - m_new), jnp.exp2(s - m_new).astype(BF16)

    cols = [slice(c * ATT_CW, (c + 1) * ATT_CW) for c in range(ATT_NCT)]

    def score(ki, c):
        return _dot_nt(keys(ki), q_ref[0, 0, cols[c], :])

    def run_stages(stages, m, acc):
        m, acc = list(m), list(acc)
        s_cur = [score(stages[0][0], c) if c >= stages[0][1] else None for c in range(ATT_NCT)]
        for idx, (ki, c0, masked) in enumerate(stages):
            nxt = stages[idx + 1] if idx + 1 < len(stages) else None
            stats = {}
            for c in range(c0, ATT_NCT):
                t = s_cur[c]
                if masked and c == c0:
                    t = jnp.where(pcol_ref[0, cols[c0], :] <= prow_ref[0, :, cols[c]], t, NEG_BIG)
                stats[c] = soft(t, m[c])
            vtb = values_t(ki)
            s_nxt = [None] * ATT_NCT
            for c in range(c0, ATT_NCT):
                if nxt is not None and c >= nxt[1]:
                    s_nxt[c] = score(nxt[0], c)
                m[c] = stats[c][0]
                acc[c] = stats[c][1] * acc[c] + _dot(vtb, stats[c][2])
            s_cur = s_nxt
        return tuple(m), tuple(acc)

    def visible_blocks(i, carry):
        base = ATT_KB_PER_Q * i
        return run_stages([(base + j, 0, False) for j in range(ATT_KB_PER_Q)], *carry)

    assert ATT_TK == ATT_CW
    init = (tuple(jnp.full((1, ATT_CW), NEG_BIG, F32) for _ in range(ATT_NCT)),
            tuple(jnp.zeros((VT_ROWS, ATT_CW), F32) for _ in range(ATT_NCT)))
    m, acc = lax.fori_loop(0, qi, visible_blocks, init)
    base = ATT_KB_PER_Q * qi
    _, acc = run_stages([(base + j, j, True) for j in range(ATT_KB_PER_Q)], m, acc)
    acc = jnp.concatenate(acc, axis=1)
    o_ref[0, 0] = (acc[:V_DIM] / acc[V_DIM:V_DIM + 1]).astype(BF16)


def _attention(q, k, vt, pos_col, pos_row):
    nq = SEQ // ATT_TQ
    return pl.pallas_call(
        _attn_kernel,
        grid=(BATCH, N_HEADS, nq),
        in_specs=[
            pl.BlockSpec((1, 1, ATT_TQ, HEAD_PAD), lambda b, h, i: (b, h, i, 0)),
            pl.BlockSpec((1, 1, SEQ, HEAD_PAD), lambda b, h, i: (b, h, 0, 0)),
            pl.BlockSpec((1, 1, SEQ // IN_TM, VT_ROWS, IN_TM), lambda b, h, i: (b, h, 0, 0, 0)),
            pl.BlockSpec((1, ATT_TQ, 1), lambda b, h, i: (b, i, 0)),
            pl.BlockSpec((1, 1, ATT_TQ), lambda b, h, i: (b, 0, i)),
        ],
        out_specs=pl.BlockSpec((1, 1, V_DIM, ATT_TQ), lambda b, h, i: (b, h, 0, i)),
        out_shape=jax.ShapeDtypeStruct((BATCH, N_HEADS, V_DIM, SEQ), BF16),
        compiler_params=_cparams(3),
        name="mla_attention",
    )(q, k, vt, pos_col, pos_row)


MIX_TM = 512


def _gelu_tanh(x):
    return x * (0.5 * (1.0 + jnp.tanh(math.sqrt(2.0 / math.pi) * (x + 0.044715 * (x * x * x)))))


def _mix_kernel(x_ref, y_ref, o_ref, ga_ref, gb_ref, mod_ref, wglu_ref, bglu_ref, wa_ref, wb_ref, wo_ref, out_ref):
    y = _gelu_tanh(y_ref[...])
    gl = _dot(y.astype(BF16), wglu_ref[...]) + bglu_ref[...]
    ya = _dot((y * _sigmoid(gl)).astype(BF16), wa_ref[...])
    ot = jnp.concatenate([o_ref[0, hd] for hd in range(N_HEADS)], axis=0)
    yb = lax.dot_general(ot, wb_ref[...], (((0,), (0,)), ((), ())), preferred_element_type=F32)
    merged = ga_ref[...] * ya + gb_ref[...] * yb
    g1 = mod_ref[0][:, 2 * D_MODEL:3 * D_MODEL]
    out_ref[...] = x_ref[...] + g1 * _dot(merged.astype(BF16), wo_ref[...])


def _mix(x, y_ssm, o, ga, gb, mod_l, wglu, bglu, wa, wb, wo):
    tm = MIX_TM
    nt = SEQ // tm
    const = _resident
    tok = lambda width: pl.BlockSpec((tm, width), lambda i: (i, 0))
    return pl.pallas_call(
        _mix_kernel,
        grid=(TOKENS // tm,),
        in_specs=[
            tok(D_MODEL), tok(SSM_WIDTH),
            pl.BlockSpec((1, N_HEADS, V_DIM, tm), lambda i: (i // nt, 0, 0, i % nt)),
            tok(D_MODEL), tok(D_MODEL),
            pl.BlockSpec((1, 1, N_MOD * D_MODEL), lambda i: (i // nt, 0, 0)),
            const((SSM_WIDTH, SSM_WIDTH)), const((1, SSM_WIDTH)),
            const((SSM_WIDTH, D_MODEL)), const((N_HEADS * V_DIM, D_MODEL)), const((D_MODEL, D_MODEL)),
        ],
        out_specs=tok(D_MODEL),
        out_shape=jax.ShapeDtypeStruct((TOKENS, D_MODEL), F32),
        compiler_params=_cparams(1),
        name="branch_mix",
    )(x, y_ssm, o, ga, gb, mod_l, wglu, bglu, wa, wb, wo)


FFN_TM = 512


def _ffn_kernel(x_ref, mod_ref, g2_ref, wg_ref, wu_ref, wd_ref, fg_ref, out_ref, *, final_norm):
    x = x_ref[...]
    mod = mod_ref[0]
    sh = mod[:, 3 * D_MODEL:4 * D_MODEL]
    sc = mod[:, 4 * D_MODEL:5 * D_MODEL]
    xn = x * lax.rsqrt(jnp.mean(x * x, axis=-1, keepdims=True) + EPS) * g2_ref[...]
    h = (xn * (1.0 + sc) + sh).astype(BF16)
    g = _dot(h, wg_ref[...])
    act = (g * _sigmoid(g)) * _dot(h, wu_ref[...])
    y = x + mod[:, 5 * D_MODEL:6 * D_MODEL] * _dot(act.astype(BF16), wd_ref[...])
    if final_norm:
        y = y * lax.rsqrt(jnp.mean(y * y, axis=-1, keepdims=True) + EPS) * fg_ref[...]
    out_ref[...] = y


def _ffn(x, mod_l, g2, wg, wu, wd, fg, final_norm):
    tm = FFN_TM
    nt = SEQ // tm
    return pl.pallas_call(
        functools.partial(_ffn_kernel, final_norm=final_norm),
        grid=(TOKENS // tm,),
        in_specs=[
            pl.BlockSpec((tm, D_MODEL), lambda i: (i, 0)),
            pl.BlockSpec((1, 1, N_MOD * D_MODEL), lambda i: (i // nt, 0, 0)),
            _resident((1, D_MODEL)),
            _resident((D_MODEL, D_FF)), _resident((D_MODEL, D_FF)), _resident((D_FF, D_MODEL)),
            _resident((1, D_MODEL)),
        ],
        out_specs=pl.BlockSpec((tm, D_MODEL), lambda i: (i, 0)),
        out_shape=jax.ShapeDtypeStruct((TOKENS, D_MODEL), F32),
        compiler_params=_cparams(1),
        name="swiglu_ffn",
    )(x, mod_l, g2, wg, wu, wd, fg)


def _pad_heads(w, dim):
    k = w.shape[0]
    w = w.reshape(k, N_HEADS, dim)
    return jnp.pad(w, ((0, 0), (0, 0), (0, HEAD_PAD - dim))).reshape(k, N_HEADS * HEAD_PAD)


def _swap_rope_halves(w_rope):
    half = QK_ROPE // 2
    return jnp.concatenate([w_rope[..., half:], w_rope[..., :half]], axis=-1)


def kernel(x, c, positions, w_ada, b_ada, norm1_g, w_in, ssm_a_re, ssm_a_im, ssm_log_dt, ssm_b_re, ssm_b_im,
           ssm_c_re, ssm_c_im, ssm_d, w_glu, b_glu, w_a_out, q_norm_g, w_uq, kv_norm_g, w_uk, w_uv, w_b_out,
           w_out, norm2_g, w_gate, w_up, w_down, final_g):
    inv_freq = ROPE_BASE ** (-jnp.arange(0, QK_ROPE, 2, dtype=F32) / QK_ROPE)
    half = QK_ROPE // 2
    freq_lanes = jnp.zeros((1, LANES), F32).at[0, QK_NOPE:QK_DIM].set(jnp.tile(inv_freq, 2))
    sign_lanes = (jnp.zeros((1, LANES), F32).at[0, QK_NOPE:QK_NOPE + half].set(-1.0)
                  .at[0, QK_NOPE + half:QK_DIM].set(1.0))
    cos_t, sin_t = _rope_tables(positions, freq_lanes, sign_lanes)
    pos_col = positions.reshape(BATCH, SEQ, 1)
    pos_row = positions.reshape(BATCH, 1, SEQ)

    mods = _adaln_mods(jnp.pad(c, ((0, 8 - BATCH), (0, 0))), w_ada, b_ada)
    m_mat, ws, wot, dec = _ssm_prepare(ssm_a_re, ssm_a_im, ssm_log_dt, ssm_b_re, ssm_b_im, ssm_c_re, ssm_c_im)
    d_skip = ssm_d.reshape(DEPTH, SSM_QBLOCKS, 1, LANES)

    xt = x.reshape(TOKENS, D_MODEL)
    for l in range(DEPTH):
        mod_l = mods[l, :BATCH].reshape(BATCH, 1, N_MOD * D_MODEL)
        wi = w_in[l]
        o_u, o_cq, o_ckv, o_kr, o_ga = (SSM_WIDTH, SSM_WIDTH + Q_LORA, SSM_WIDTH + Q_LORA + KV_LORA,
                                         SSM_WIDTH + Q_LORA + KV_LORA + QK_ROPE,
                                         SSM_WIDTH + Q_LORA + KV_LORA + QK_ROPE + D_MODEL)
        w_kr = wi[:, o_ckv:o_kr]
        rope_pad = ((0, 0), (QK_NOPE, HEAD_PAD - QK_DIM))
        w_all = jnp.concatenate([
            wi[:, :o_ckv],
            jnp.pad(w_kr, rope_pad), jnp.pad(_swap_rope_halves(w_kr), rope_pad),
            wi[:, o_kr:],
        ], axis=1).astype(BF16)
        uq = w_uq[l].reshape(Q_LORA, N_HEADS, QK_DIM)
        uq_swapped = jnp.concatenate([jnp.zeros_like(uq[..., :QK_NOPE]), _swap_rope_halves(uq[..., QK_NOPE:])], axis=-1)
        wq1 = _pad_heads(w_uq[l], QK_DIM).astype(BF16)
        wq2 = _pad_heads(uq_swapped.reshape(Q_LORA, N_HEADS * QK_DIM), QK_DIM).astype(BF16)
        wk = _pad_heads(w_uk[l], QK_NOPE).astype(BF16)
        wvt = w_uv[l].T.astype(BF16)

        u, q, k, vt, ga, gb = _in_proj(xt, mod_l, norm1_g[l].reshape(1, D_MODEL), w_all,
                                      q_norm_g[l].reshape(1, Q_LORA), kv_norm_g[l].reshape(1, KV_LORA),
                                      wq1, wq2, wk, wvt, cos_t, sin_t)
        y_ssm = _ssm_scan(u, m_mat, ws, wot, dec, d_skip, l)
        o = _attention(q, k, vt, pos_col, pos_row)
        xt = _mix(xt, y_ssm, o, ga, gb, mod_l, w_glu[l].astype(BF16), b_glu[l].reshape(1, SSM_WIDTH),
                  w_a_out[l].astype(BF16), w_b_out[l].astype(BF16), w_out[l].astype(BF16))
        xt = _ffn(xt, mod_l, norm2_g[l].reshape(1, D_MODEL), w_gate[l].astype(BF16), w_up[l].astype(BF16),
                  w_down[l].astype(BF16), final_g.reshape(1, D_MODEL), final_norm=(l == DEPTH - 1))
    return xt.reshape(BATCH, SEQ, D_MODEL)
```

```python
import functools
import math

import jax
import jax.numpy as jnp
import numpy as np
from jax import lax
from jax.experimental import pallas as pl
from jax.experimental.pallas import tpu as pltpu

D_MODEL = 1024
BATCH = 2
SEQ = 16384
DEPTH = 4
TOKENS = BATCH * SEQ

SSM_WIDTH = D_MODEL // 2
SSM_GROUP = 16
SSM_GROUPS = SSM_WIDTH // SSM_GROUP
SSM_STATE = 64
N_HEADS = 8
QK_NOPE = 64
QK_ROPE = 32
QK_DIM = QK_NOPE + QK_ROPE
V_DIM = 64
Q_LORA = 256
KV_LORA = 128
ROPE_BASE = 10000.0
D_FF = 2816
N_MOD = 6
EPS = 1e-6

LANES = 128
HEAD_PAD = LANES
SSM_CHUNK = 16
GROUPS_PER_TILE = LANES // SSM_GROUP
SSM_QBLOCKS = SSM_WIDTH // LANES
STATE_LANES = GROUPS_PER_TILE * SSM_STATE
CHUNK_LANES = SSM_CHUNK * LANES
VMEM_LIMIT = 56 * 1024 * 1024

F32 = jnp.float32
BF16 = jnp.bfloat16
NEG_BIG = -1e30


def _cparams(n_axes, vmem=VMEM_LIMIT, flags=None):
    return pltpu.CompilerParams(dimension_semantics=("arbitrary",) * n_axes, vmem_limit_bytes=vmem, flags=flags)


def _resident(shape):
    return pl.BlockSpec(shape, lambda *_: (0,) * len(shape), pipeline_mode=pl.Buffered(1))


def _sigmoid(x):
    return 1.0 / (1.0 + jnp.exp(-x))


def _split_bf16(a):
    hi = a.astype(BF16)
    lo = (a - hi.astype(F32)).astype(BF16)
    return hi, lo


def _dot(a, b):
    return jnp.dot(a, b, preferred_element_type=F32)


def _dot_nt(a, b):
    return lax.dot_general(a, b, (((1,), (1,)), ((), ())), preferred_element_type=F32)


def _dot3(a, b, nt=False):
    f = _dot_nt if nt else _dot
    ah, al = _split_bf16(a)
    bh, bl = _split_bf16(b)
    return f(ah, bh) + (f(al, bh) + f(ah, bl))


def _mod_kernel(c_ref, w_ref, b_ref, o_ref):
    c = c_ref[...]
    c_act = c * _sigmoid(c)
    o_ref[0] = _dot3(c_act, w_ref[0]) + b_ref[0]


def _adaln_mods(c_pad, w_ada, b_ada):
    ncol = N_MOD * D_MODEL
    blk = D_MODEL
    return pl.pallas_call(
        _mod_kernel,
        grid=(DEPTH, ncol // blk),
        in_specs=[
            pl.BlockSpec((8, D_MODEL), lambda l, j: (0, 0)),
            pl.BlockSpec((1, D_MODEL, blk), lambda l, j: (l, 0, j)),
            pl.BlockSpec((1, 1, blk), lambda l, j: (l, 0, j)),
        ],
        out_specs=pl.BlockSpec((1, 8, blk), lambda l, j: (l, 0, j)),
        out_shape=jax.ShapeDtypeStruct((DEPTH, 8, ncol), F32),
        compiler_params=_cparams(2),
        name="adaln_mod",
    )(c_pad, w_ada, b_ada.reshape(DEPTH, 1, ncol))


def _rope_kernel(pos_ref, freq_ref, sign_ref, c_ref, s_ref):
    ang = pos_ref[0].astype(F32) * freq_ref[...]
    c_ref[0] = jnp.cos(ang)
    s_ref[0] = jnp.sin(ang) * sign_ref[...]


def _rope_tables(positions, freq_lanes, sign_lanes):
    tm = 2048
    spec = pl.BlockSpec((1, tm, LANES), lambda b, i: (b, i, 0))
    vec = pl.BlockSpec((1, LANES), lambda b, i: (0, 0))
    return pl.pallas_call(
        _rope_kernel,
        grid=(BATCH, SEQ // tm),
        in_specs=[pl.BlockSpec((1, tm, 1), lambda b, i: (b, i, 0)), vec, vec],
        out_specs=[spec, spec],
        out_shape=[jax.ShapeDtypeStruct((BATCH, SEQ, LANES), F32)] * 2,
        compiler_params=_cparams(2),
        name="rope_tables",
    )(positions.reshape(BATCH, SEQ, 1), freq_lanes, sign_lanes)


def _ssm_prep_kernel(ar_ref, ai_ref, ldt_ref, bre_ref, bim_ref, cre_ref, cim_ref,
                     m_ref, ws_ref, wot_ref, dec_ref):
    ar = ar_ref[0, 0]
    ai = ai_ref[0, 0]
    dt = jnp.exp(ldt_ref[0, 0])
    mag = jnp.exp(ar * dt)
    ab_re = mag * jnp.cos(ai * dt)
    ab_im = mag * jnp.sin(ai * dt)
    den = ar * ar + ai * ai
    nr = ab_re - 1.0
    ni = ab_im
    coef_re = (nr * ar + ni * ai) / den
    coef_im = (ni * ar - nr * ai) / den
    b_re = bre_ref[0, 0]
    b_im = bim_ref[0, 0]
    bb_re = coef_re * b_re - coef_im * b_im
    bb_im = coef_re * b_im + coef_im * b_re
    c_re = cre_ref[0, 0]
    c_im = cim_ref[0, 0]

    pw = [(jnp.ones_like(ab_re), jnp.zeros_like(ab_im))]
    for _ in range(SSM_CHUNK):
        pr, pi = pw[-1]
        pw.append((pr * ab_re - pi * ab_im, pr * ab_im + pi * ab_re))

    bb_cat = jnp.concatenate([bb_re, -bb_im], axis=1)
    r0 = []
    for j in range(SSM_CHUNK + 1):
        pr, pi = pw[j]
        e_re = c_re * pr - c_im * pi
        e_im = c_re * pi + c_im * pr
        if j < SSM_CHUNK:
            r0.append(_dot3(bb_cat, jnp.concatenate([e_re, e_im], axis=1), nt=True).astype(BF16))
            s = SSM_CHUNK - 1 - j
            ws_ref[0, 0, s * LANES:(s + 1) * LANES, :] = jnp.concatenate(
                [bb_re * pr - bb_im * pi, bb_re * pi + bb_im * pr], axis=1).astype(BF16)
        if j >= 1:
            wot_ref[0, 0, (j - 1) * LANES:j * LANES, :] = jnp.concatenate([e_re, -e_im], axis=1).astype(BF16)
    zero = jnp.zeros((LANES, LANES), BF16)
    for s in range(SSM_CHUNK):
        for j in range(SSM_CHUNK):
            m_ref[0, 0, s * LANES:(s + 1) * LANES, j * LANES:(j + 1) * LANES] = r0[j - s] if j >= s else zero
    pr, pi = pw[SSM_CHUNK]
    dec_ref[0, 0] = jnp.concatenate([pr, pi], axis=1)


def _ssm_prepare(a_re, a_im, log_dt, b_re, b_im, c_re, c_im):
    nq, gt, p, m = SSM_QBLOCKS, GROUPS_PER_TILE, SSM_STATE, SSM_GROUP
    eye = jnp.eye(gt, dtype=F32)

    def lanes(v):
        return v.reshape(DEPTH, nq, 1, gt * p)

    def place(v):
        rows = v.shape[3]
        out = v[:, :, :, :, None, :] * eye[None, None, :, None, :, None]
        return out.reshape(DEPTH, nq, gt * rows, gt * p)

    ldt = jnp.broadcast_to(log_dt[:, :, None], (DEPTH, SSM_GROUPS, p))
    bt_re = place(jnp.swapaxes(b_re, 2, 3).reshape(DEPTH, nq, gt, m, p))
    bt_im = place(jnp.swapaxes(b_im, 2, 3).reshape(DEPTH, nq, gt, m, p))
    ct_re = place(c_re.reshape(DEPTH, nq, gt, m, p))
    ct_im = place(c_im.reshape(DEPTH, nq, gt, m, p))

    vec = pl.BlockSpec((1, 1, 1, STATE_LANES), lambda l, q: (l, q, 0, 0))
    mat = pl.BlockSpec((1, 1, LANES, STATE_LANES), lambda l, q: (l, q, 0, 0))
    return pl.pallas_call(
        _ssm_prep_kernel,
        grid=(DEPTH, nq),
        in_specs=[vec, vec, vec, mat, mat, mat, mat],
        out_specs=[
            pl.BlockSpec((1, 1, CHUNK_LANES, CHUNK_LANES), lambda l, q: (l, q, 0, 0)),
            pl.BlockSpec((1, 1, CHUNK_LANES, 2 * STATE_LANES), lambda l, q: (l, q, 0, 0)),
            pl.BlockSpec((1, 1, CHUNK_LANES, 2 * STATE_LANES), lambda l, q: (l, q, 0, 0)),
            pl.BlockSpec((1, 1, 1, 2 * STATE_LANES), lambda l, q: (l, q, 0, 0)),
        ],
        out_shape=[
            jax.ShapeDtypeStruct((DEPTH, nq, CHUNK_LANES, CHUNK_LANES), BF16),
            jax.ShapeDtypeStruct((DEPTH, nq, CHUNK_LANES, 2 * STATE_LANES), BF16),
            jax.ShapeDtypeStruct((DEPTH, nq, CHUNK_LANES, 2 * STATE_LANES), BF16),
            jax.ShapeDtypeStruct((DEPTH, nq, 1, 2 * STATE_LANES), F32),
        ],
        compiler_params=_cparams(2),
        name="ssm_prepare",
    )(lanes(a_re), lanes(a_im), lanes(ldt), bt_re, bt_im, ct_re, ct_im)


SSM_ROWS = 128
SSM_TOK = SSM_ROWS * SSM_CHUNK


SSM_MT = 256


def _ssm_kernel(u_ref, m_ref, ws_ref, wot_ref, dec_ref, d_ref, y_ref, s_scr, h_scr, carry_scr):
    r = pl.program_id(1)

    @pl.when(r == 0)
    def _():
        carry_scr[...] = jnp.zeros_like(carry_scr)

    x = jnp.concatenate(
        [jnp.concatenate([u_ref[b, pl.ds(s, SSM_ROWS, stride=SSM_CHUNK), :] for s in range(SSM_CHUNK)], axis=1)
         for b in range(BATCH)], axis=0)
    xb = x.astype(BF16)
    s_scr[...] = _dot(xb, ws_ref[0, 0])

    dec = dec_ref[0, 0]
    d_re = dec[:, :STATE_LANES]
    d_im = dec[:, STATE_LANES:]

    def body(i, hs):
        out = []
        for b in range(BATCH):
            base = pl.multiple_of(b * SSM_ROWS + i * 8, 8)
            h = hs[b]
            blk = s_scr[pl.ds(base, 8), :]
            rows = []
            for k in range(8):
                rows.append(h)
                h_re = h[:, :STATE_LANES]
                h_im = h[:, STATE_LANES:]
                inc = blk[k:k + 1, :]
                n_re = d_re * h_re - d_im * h_im + inc[:, :STATE_LANES]
                n_im = d_re * h_im + d_im * h_re + inc[:, STATE_LANES:]
                h = jnp.concatenate([n_re, n_im], axis=1)
            h_scr[pl.ds(base, 8), :] = jnp.concatenate(rows, axis=0)
            out.append(h)
        return tuple(out)

    hs = lax.fori_loop(0, SSM_ROWS // 8, body, tuple(carry_scr[b] for b in range(BATCH)))
    for b in range(BATCH):
        carry_scr[b] = hs[b]

    d_skip = jnp.concatenate([d_ref[0, 0]] * SSM_CHUNK, axis=1)
    y_state = _dot_nt(h_scr[...].astype(BF16), wot_ref[0, 0]) + d_skip * x
    for t in range(CHUNK_LANES // SSM_MT):
        lo, hi = t * SSM_MT, (t + 1) * SSM_MT
        y_t = _dot(xb[:, :hi], m_ref[0, 0, :hi, lo:hi]) + y_state[:, lo:hi]
        for j in range(lo // LANES, hi // LANES):
            for b in range(BATCH):
                y_ref[b, pl.ds(j, SSM_ROWS, stride=SSM_CHUNK), :] = (
                    y_t[b * SSM_ROWS:(b + 1) * SSM_ROWS, j * LANES - lo:(j + 1) * LANES - lo])


def _ssm_scan(u, m_mat, ws, wot, dec, d_skip, layer):
    nr = SEQ // SSM_TOK
    tok = pl.BlockSpec((BATCH, SSM_TOK, LANES), lambda q, r: (0, r, q))
    return pl.pallas_call(
        _ssm_kernel,
        grid=(SSM_QBLOCKS, nr),
        in_specs=[
            tok,
            pl.BlockSpec((1, 1, CHUNK_LANES, CHUNK_LANES), lambda q, r: (layer, q, 0, 0)),
            pl.BlockSpec((1, 1, CHUNK_LANES, 2 * STATE_LANES), lambda q, r: (layer, q, 0, 0)),
            pl.BlockSpec((1, 1, CHUNK_LANES, 2 * STATE_LANES), lambda q, r: (layer, q, 0, 0)),
            pl.BlockSpec((1, 1, 1, 2 * STATE_LANES), lambda q, r: (layer, q, 0, 0)),
            pl.BlockSpec((1, 1, 1, LANES), lambda q, r: (layer, q, 0, 0)),
        ],
        out_specs=tok,
        out_shape=jax.ShapeDtypeStruct((BATCH, SEQ, SSM_WIDTH), F32),
        scratch_shapes=[
            pltpu.VMEM((BATCH * SSM_ROWS, 2 * STATE_LANES), F32),
            pltpu.VMEM((BATCH * SSM_ROWS, 2 * STATE_LANES), F32),
            pltpu.VMEM((BATCH, 1, 2 * STATE_LANES), F32),
        ],
        compiler_params=_cparams(2),
        name="ssm_scan",
    )(u.reshape(BATCH, SEQ, SSM_WIDTH), m_mat, ws, wot, dec, d_skip).reshape(TOKENS, SSM_WIDTH)


IN_TM = 512
VT_ROWS = V_DIM + 16
VT_CHUNK = 256
_U0, _CQ0, _CKV0, _KA0, _KB0, _GA0, _GB0, _IN_END = 0, 512, 768, 896, 1024, 1152, 2176, 3200


def _in_kernel(x_ref, mod_ref, g1_ref, w_ref, qg_ref, kvg_ref, wq1_ref, wq2_ref, wk_ref, wvt_ref,
               c_ref, s_ref, u_ref, q_ref, k_ref, vt_ref, ga_ref, gb_ref, *, q_scale):
    x = x_ref[...]
    mod = mod_ref[0]
    sh = mod[:, 0:D_MODEL]
    sc = mod[:, D_MODEL:2 * D_MODEL]
    xn = x * lax.rsqrt(jnp.mean(x * x, axis=-1, keepdims=True) + EPS) * g1_ref[...]
    h = (xn * (1.0 + sc) + sh).astype(BF16)
    z = _dot(h, w_ref[...])

    u_ref[...] = z[:, _U0:_CQ0]
    ga_ref[...] = _sigmoid(z[:, _GA0:_GB0]).astype(BF16)
    gb_ref[...] = _sigmoid(z[:, _GB0:_IN_END]).astype(BF16)

    cos = c_ref[0]
    sin = s_ref[0]
    cq = z[:, _CQ0:_CKV0]
    cqn = (cq * lax.rsqrt(jnp.mean(cq * cq, axis=-1, keepdims=True) + EPS) * qg_ref[...]).astype(BF16)
    qa = _dot(cqn, wq1_ref[...])
    qb = _dot(cqn, wq2_ref[...])
    ckv = z[:, _CKV0:_KA0]
    ckvn = (ckv * lax.rsqrt(jnp.mean(ckv * ckv, axis=-1, keepdims=True) + EPS) * kvg_ref[...]).astype(BF16)
    kn = _dot(ckvn, wk_ref[...])
    vvt = _dot_nt(wvt_ref[...], ckvn)
    k_pe = z[:, _KA0:_KB0] * cos + z[:, _KB0:_GA0] * sin
    for hd in range(N_HEADS):
        sl = slice(hd * HEAD_PAD, (hd + 1) * HEAD_PAD)
        q_ref[0, hd] = ((qa[:, sl] * cos + qb[:, sl] * sin) * q_scale).astype(BF16)
        k_ref[0, hd] = (kn[:, sl] + k_pe).astype(BF16)
        for ch in range(IN_TM // VT_CHUNK):
            lanes = slice(ch * VT_CHUNK, (ch + 1) * VT_CHUNK)
            vt_ref[0, hd, ch, :V_DIM, :] = vvt[hd * V_DIM:(hd + 1) * V_DIM, lanes].astype(BF16)
            vt_ref[0, hd, ch, V_DIM:, :] = jnp.ones((VT_ROWS - V_DIM, VT_CHUNK), BF16)


def _in_proj(x, mod_l, g1, w_all, qg, kvg, wq1, wq2, wk, wvt, cos_t, sin_t):
    tm = IN_TM
    assert ATT_TK % VT_CHUNK == 0 and tm % VT_CHUNK == 0
    nt = SEQ // tm
    const = _resident
    tok = lambda width: pl.BlockSpec((tm, width), lambda i: (i, 0))
    head = pl.BlockSpec((1, N_HEADS, tm, HEAD_PAD), lambda i: (i // nt, 0, i % nt, 0))
    tab = pl.BlockSpec((1, tm, LANES), lambda i: (i // nt, i % nt, 0))
    hshape = jax.ShapeDtypeStruct((BATCH, N_HEADS, SEQ, HEAD_PAD), BF16)
    q_scale = QK_DIM ** -0.5 * math.log2(math.e)
    return pl.pallas_call(
        functools.partial(_in_kernel, q_scale=q_scale),
        grid=(TOKENS // tm,),
        in_specs=[
            tok(D_MODEL),
            pl.BlockSpec((1, 1, N_MOD * D_MODEL), lambda i: (i // nt, 0, 0)),
            const((1, D_MODEL)),
            const((D_MODEL, _IN_END)),
            const((1, Q_LORA)), const((1, KV_LORA)),
            const((Q_LORA, N_HEADS * HEAD_PAD)), const((Q_LORA, N_HEADS * HEAD_PAD)),
            const((KV_LORA, N_HEADS * HEAD_PAD)), const((N_HEADS * V_DIM, KV_LORA)),
            tab, tab,
        ],
        out_specs=[tok(SSM_WIDTH), head, head,
                   pl.BlockSpec((1, N_HEADS, tm // VT_CHUNK, VT_ROWS, VT_CHUNK),
                                lambda i: (i // nt, 0, i % nt, 0, 0)),
                   tok(D_MODEL), tok(D_MODEL)],
        out_shape=[
            jax.ShapeDtypeStruct((TOKENS, SSM_WIDTH), F32), hshape, hshape,
            jax.ShapeDtypeStruct((BATCH, N_HEADS, SEQ // VT_CHUNK, VT_ROWS, VT_CHUNK), BF16),
            jax.ShapeDtypeStruct((TOKENS, D_MODEL), BF16), jax.ShapeDtypeStruct((TOKENS, D_MODEL), BF16),
        ],
        compiler_params=_cparams(1),
        name="in_proj",
    )(x, mod_l, g1, w_all, qg, kvg, wq1, wq2, wk, wvt, cos_t, sin_t)


ATT_TQ = 4096
ATT_TK = 256
ATT_KB_PER_Q = ATT_TQ // ATT_TK
ATT_CW = 256
ATT_NCT = ATT_TQ // ATT_CW


def _attn_kernel(q_ref, k_ref, vt_ref, pcol_ref, prow_ref, o_ref):
    qi = pl.program_id(2)

    def keys(ki):
        return k_ref[0, 0, pl.ds(pl.multiple_of(ki * ATT_TK, ATT_TK), ATT_TK), :]

    def values_t(ki):
        nchunk = ATT_TK // VT_CHUNK
        return jnp.concatenate([vt_ref[0, 0, ki * nchunk + c] for c in range(nchunk)], axis=1)

    def soft(s, m_old):
        m_new = jnp.maximum(m_old, jnp.max(s, axis=0, keepdims=True))
        return m_new, jnp.exp2(m_old - m_new), jnp.exp2(s - m_new).astype(BF16)

    cols = [slice(c * ATT_CW, (c + 1) * ATT_CW) for c in range(ATT_NCT)]

    def score(ki, c):
        return _dot_nt(keys(ki), q_ref[0, 0, cols[c], :])

    def run_stages(stages, m, acc):
        m, acc = list(m), list(acc)
        s_cur = [score(stages[0][0], c) if c >= stages[0][1] else None for c in range(ATT_NCT)]
        for idx, (ki, c0, key_rows) in enumerate(stages):
            nxt = stages[idx + 1] if idx + 1 < len(stages) else None
            stats = {}
            for c in range(c0, ATT_NCT):
                t = s_cur[c]
                if key_rows is not None and c == c0:
                    t = jnp.where(pcol_ref[0, key_rows, :] <= prow_ref[0, :, cols[c]], t, NEG_BIG)
                stats[c] = soft(t, m[c])
            vtb = values_t(ki)
            s_nxt = [None] * ATT_NCT
            for c in range(c0, ATT_NCT):
                if nxt is not None and c >= nxt[1]:
                    s_nxt[c] = score(nxt[0], c)
                m[c] = stats[c][0]
                acc[c] = stats[c][1] * acc[c] + _dot(vtb, stats[c][2])
            s_cur = s_nxt
        return tuple(m), tuple(acc)

    def visible_blocks(i, carry):
        base = ATT_KB_PER_Q * i
        return run_stages([(base + j, 0, None) for j in range(ATT_KB_PER_Q)], *carry)

    init = (tuple(jnp.full((1, ATT_CW), NEG_BIG, F32) for _ in range(ATT_NCT)),
            tuple(jnp.zeros((VT_ROWS, ATT_CW), F32) for _ in range(ATT_NCT)))
    m, acc = lax.fori_loop(0, qi, visible_blocks, init)
    base = ATT_KB_PER_Q * qi
    span = [(base + j, (j * ATT_TK) // ATT_CW, slice(j * ATT_TK, (j + 1) * ATT_TK)) for j in range(ATT_KB_PER_Q)]
    _, acc = run_stages(span, m, acc)
    acc = jnp.concatenate(acc, axis=1)
    o_ref[0, 0] = (acc[:V_DIM] / acc[V_DIM:V_DIM + 1]).astype(BF16)


def _attention(q, k, vt, pos_col, pos_row):
    nq = SEQ // ATT_TQ
    return pl.pallas_call(
        _attn_kernel,
        grid=(BATCH, N_HEADS, nq),
        in_specs=[
            pl.BlockSpec((1, 1, ATT_TQ, HEAD_PAD), lambda b, h, i: (b, h, i, 0)),
            pl.BlockSpec((1, 1, SEQ, HEAD_PAD), lambda b, h, i: (b, h, 0, 0)),
            pl.BlockSpec((1, 1, SEQ // VT_CHUNK, VT_ROWS, VT_CHUNK), lambda b, h, i: (b, h, 0, 0, 0)),
            pl.BlockSpec((1, ATT_TQ, 1), lambda b, h, i: (b, i, 0)),
            pl.BlockSpec((1, 1, ATT_TQ), lambda b, h, i: (b, 0, i)),
        ],
        out_specs=pl.BlockSpec((1, 1, V_DIM, ATT_TQ), lambda b, h, i: (b, h, 0, i)),
        out_shape=jax.ShapeDtypeStruct((BATCH, N_HEADS, V_DIM, SEQ), BF16),
        compiler_params=_cparams(3),
        name="mla_attention",
    )(q, k, vt, pos_col, pos_row)


MIX_TM = 512


def _gelu_tanh(x):
    return x * (0.5 * (1.0 + jnp.tanh(math.sqrt(2.0 / math.pi) * (x + 0.044715 * (x * x * x)))))


def _mix_kernel(x_ref, y_ref, o_ref, ga_ref, gb_ref, mod_ref, wglu_ref, bglu_ref, wa_ref, wb_ref, wo_ref, out_ref):
    y = _gelu_tanh(y_ref[...])
    gl = _dot(y.astype(BF16), wglu_ref[...]) + bglu_ref[...]
    ya = _dot((y * _sigmoid(gl)).astype(BF16), wa_ref[...])
    ot = jnp.concatenate([o_ref[0, hd] for hd in range(N_HEADS)], axis=0)
    yb = lax.dot_general(ot, wb_ref[...], (((0,), (0,)), ((), ())), preferred_element_type=F32)
    merged = ga_ref[...] * ya + gb_ref[...] * yb
    g1 = mod_ref[0][:, 2 * D_MODEL:3 * D_MODEL]
    out_ref[...] = x_ref[...] + g1 * _dot(merged.astype(BF16), wo_ref[...])


def _mix(x, y_ssm, o, ga, gb, mod_l, wglu, bglu, wa, wb, wo):
    tm = MIX_TM
    nt = SEQ // tm
    const = _resident
    tok = lambda width: pl.BlockSpec((tm, width), lambda i: (i, 0))
    return pl.pallas_call(
        _mix_kernel,
        grid=(TOKENS // tm,),
        in_specs=[
            tok(D_MODEL), tok(SSM_WIDTH),
            pl.BlockSpec((1, N_HEADS, V_DIM, tm), lambda i: (i // nt, 0, 0, i % nt)),
            tok(D_MODEL), tok(D_MODEL),
            pl.BlockSpec((1, 1, N_MOD * D_MODEL), lambda i: (i // nt, 0, 0)),
            const((SSM_WIDTH, SSM_WIDTH)), const((1, SSM_WIDTH)),
            const((SSM_WIDTH, D_MODEL)), const((N_HEADS * V_DIM, D_MODEL)), const((D_MODEL, D_MODEL)),
        ],
        out_specs=tok(D_MODEL),
        out_shape=jax.ShapeDtypeStruct((TOKENS, D_MODEL), F32),
        compiler_params=_cparams(1),
        name="branch_mix",
    )(x, y_ssm, o, ga, gb, mod_l, wglu, bglu, wa, wb, wo)


FFN_TM = 512


def _ffn_kernel(x_ref, mod_ref, g2_ref, wg_ref, wu_ref, wd_ref, fg_ref, out_ref, *, final_norm):
    x = x_ref[...]
    mod = mod_ref[0]
    sh = mod[:, 3 * D_MODEL:4 * D_MODEL]
    sc = mod[:, 4 * D_MODEL:5 * D_MODEL]
    xn = x * lax.rsqrt(jnp.mean(x * x, axis=-1, keepdims=True) + EPS) * g2_ref[...]
    h = (xn * (1.0 + sc) + sh).astype(BF16)
    g = _dot(h, wg_ref[...])
    act = (g * _sigmoid(g)) * _dot(h, wu_ref[...])
    y = x + mod[:, 5 * D_MODEL:6 * D_MODEL] * _dot(act.astype(BF16), wd_ref[...])
    if final_norm:
        y = y * lax.rsqrt(jnp.mean(y * y, axis=-1, keepdims=True) + EPS) * fg_ref[...]
    out_ref[...] = y


def _ffn(x, mod_l, g2, wg, wu, wd, fg, final_norm):
    tm = FFN_TM
    nt = SEQ // tm
    return pl.pallas_call(
        functools.partial(_ffn_kernel, final_norm=final_norm),
        grid=(TOKENS // tm,),
        in_specs=[
            pl.BlockSpec((tm, D_MODEL), lambda i: (i, 0)),
            pl.BlockSpec((1, 1, N_MOD * D_MODEL), lambda i: (i // nt, 0, 0)),
            _resident((1, D_MODEL)),
            _resident((D_MODEL, D_FF)), _resident((D_MODEL, D_FF)), _resident((D_FF, D_MODEL)),
            _resident((1, D_MODEL)),
        ],
        out_specs=pl.BlockSpec((tm, D_MODEL), lambda i: (i, 0)),
        out_shape=jax.ShapeDtypeStruct((TOKENS, D_MODEL), F32),
        compiler_params=_cparams(1),
        name="swiglu_ffn",
    )(x, mod_l, g2, wg, wu, wd, fg)


def _pad_heads(w, dim):
    k = w.shape[0]
    w = w.reshape(k, N_HEADS, dim)
    return jnp.pad(w, ((0, 0), (0, 0), (0, HEAD_PAD - dim))).reshape(k, N_HEADS * HEAD_PAD)


def _swap_rope_halves(w_rope):
    half = QK_ROPE // 2
    return jnp.concatenate([w_rope[..., half:], w_rope[..., :half]], axis=-1)


def kernel(x, c, positions, w_ada, b_ada, norm1_g, w_in, ssm_a_re, ssm_a_im, ssm_log_dt, ssm_b_re, ssm_b_im,
           ssm_c_re, ssm_c_im, ssm_d, w_glu, b_glu, w_a_out, q_norm_g, w_uq, kv_norm_g, w_uk, w_uv, w_b_out,
           w_out, norm2_g, w_gate, w_up, w_down, final_g):
    inv_freq = ROPE_BASE ** (-jnp.arange(0, QK_ROPE, 2, dtype=F32) / QK_ROPE)
    half = QK_ROPE // 2
    freq_lanes = jnp.zeros((1, LANES), F32).at[0, QK_NOPE:QK_DIM].set(jnp.tile(inv_freq, 2))
    sign_lanes = (jnp.zeros((1, LANES), F32).at[0, QK_NOPE:QK_NOPE + half].set(-1.0)
                  .at[0, QK_NOPE + half:QK_DIM].set(1.0))
    cos_t, sin_t = _rope_tables(positions, freq_lanes, sign_lanes)
    pos_col = positions.reshape(BATCH, SEQ, 1)
    pos_row = positions.reshape(BATCH, 1, SEQ)

    mods = _adaln_mods(jnp.pad(c, ((0, 8 - BATCH), (0, 0))), w_ada, b_ada)
    m_mat, ws, wot, dec = _ssm_prepare(ssm_a_re, ssm_a_im, ssm_log_dt, ssm_b_re, ssm_b_im, ssm_c_re, ssm_c_im)
    d_skip = ssm_d.reshape(DEPTH, SSM_QBLOCKS, 1, LANES)

    xt = x.reshape(TOKENS, D_MODEL)
    for l in range(DEPTH):
        mod_l = mods[l, :BATCH].reshape(BATCH, 1, N_MOD * D_MODEL)
        wi = w_in[l]
        o_u, o_cq, o_ckv, o_kr, o_ga = (SSM_WIDTH, SSM_WIDTH + Q_LORA, SSM_WIDTH + Q_LORA + KV_LORA,
                                         SSM_WIDTH + Q_LORA + KV_LORA + QK_ROPE,
                                         SSM_WIDTH + Q_LORA + KV_LORA + QK_ROPE + D_MODEL)
        w_kr = wi[:, o_ckv:o_kr]
        rope_pad = ((0, 0), (QK_NOPE, HEAD_PAD - QK_DIM))
        w_all = jnp.concatenate([
            wi[:, :o_ckv],
            jnp.pad(w_kr, rope_pad), jnp.pad(_swap_rope_halves(w_kr), rope_pad),
            wi[:, o_kr:],
        ], axis=1).astype(BF16)
        uq = w_uq[l].reshape(Q_LORA, N_HEADS, QK_DIM)
        uq_swapped = jnp.concatenate([jnp.zeros_like(uq[..., :QK_NOPE]), _swap_rope_halves(uq[..., QK_NOPE:])], axis=-1)
        wq1 = _pad_heads(w_uq[l], QK_DIM).astype(BF16)
        wq2 = _pad_heads(uq_swapped.reshape(Q_LORA, N_HEADS * QK_DIM), QK_DIM).astype(BF16)
        wk = _pad_heads(w_uk[l], QK_NOPE).astype(BF16)
        wvt = w_uv[l].T.astype(BF16)

        u, q, k, vt, ga, gb = _in_proj(xt, mod_l, norm1_g[l].reshape(1, D_MODEL), w_all,
                                      q_norm_g[l].reshape(1, Q_LORA), kv_norm_g[l].reshape(1, KV_LORA),
                                      wq1, wq2, wk, wvt, cos_t, sin_t)
        y_ssm = _ssm_scan(u, m_mat, ws, wot, dec, d_skip, l)
        o = _attention(q, k, vt, pos_col, pos_row)
        xt = _mix(xt, y_ssm, o, ga, gb, mod_l, w_glu[l].astype(BF16), b_glu[l].reshape(1, SSM_WIDTH),
                  w_a_out[l].astype(BF16), w_b_out[l].astype(BF16), w_out[l].astype(BF16))
        xt = _ffn(xt, mod_l, norm2_g[l].reshape(1, D_MODEL), w_gate[l].astype(BF16), w_up[l].astype(BF16),
                  w_down[l].astype(BF16), final_g.reshape(1, D_MODEL), final_norm=(l == DEPTH - 1))
    return xt.reshape(BATCH, SEQ, D_MODEL)
```

```python
import functools
import math

import jax
import jax.numpy as jnp
import numpy as np
from jax import lax
from jax.experimental import pallas as pl
from jax.experimental.pallas import tpu as pltpu

D_MODEL = 1024
BATCH = 2
SEQ = 16384
DEPTH = 4
TOKENS = BATCH * SEQ

SSM_WIDTH = D_MODEL // 2
SSM_GROUP = 16
SSM_GROUPS = SSM_WIDTH // SSM_GROUP
SSM_STATE = 64
N_HEADS = 8
QK_NOPE = 64
QK_ROPE = 32
QK_DIM = QK_NOPE + QK_ROPE
V_DIM = 64
Q_LORA = 256
KV_LORA = 128
ROPE_BASE = 10000.0
D_FF = 2816
N_MOD = 6
EPS = 1e-6

LANES = 128
HEAD_PAD = LANES
SSM_CHUNK = 16
GROUPS_PER_TILE = LANES // SSM_GROUP
SSM_QBLOCKS = SSM_WIDTH // LANES
STATE_LANES = GROUPS_PER_TILE * SSM_STATE
CHUNK_LANES = SSM_CHUNK * LANES
VMEM_LIMIT = 56 * 1024 * 1024

F32 = jnp.float32
BF16 = jnp.bfloat16
NEG_BIG = -1e30


def _cparams(n_axes, vmem=VMEM_LIMIT, flags=None):
    return pltpu.CompilerParams(dimension_semantics=("arbitrary",) * n_axes, vmem_limit_bytes=vmem, flags=flags)


def _resident(shape):
    return pl.BlockSpec(shape, lambda *_: (0,) * len(shape), pipeline_mode=pl.Buffered(1))


def _sigmoid(x):
    return 1.0 / (1.0 + jnp.exp(-x))


def _split_bf16(a):
    hi = a.astype(BF16)
    lo = (a - hi.astype(F32)).astype(BF16)
    return hi, lo


def _dot(a, b):
    return jnp.dot(a, b, preferred_element_type=F32)


def _dot_nt(a, b):
    return lax.dot_general(a, b, (((1,), (1,)), ((), ())), preferred_element_type=F32)


def _dot3(a, b, nt=False):
    f = _dot_nt if nt else _dot
    ah, al = _split_bf16(a)
    bh, bl = _split_bf16(b)
    return f(ah, bh) + (f(al, bh) + f(ah, bl))


def _mod_kernel(c_ref, w_ref, b_ref, o_ref):
    c = c_ref[...]
    c_act = c * _sigmoid(c)
    o_ref[0] = _dot3(c_act, w_ref[0]) + b_ref[0]


def _adaln_mods(c_pad, w_ada, b_ada):
    ncol = N_MOD * D_MODEL
    blk = D_MODEL
    return pl.pallas_call(
        _mod_kernel,
        grid=(DEPTH, ncol // blk),
        in_specs=[
            pl.BlockSpec((8, D_MODEL), lambda l, j: (0, 0)),
            pl.BlockSpec((1, D_MODEL, blk), lambda l, j: (l, 0, j)),
            pl.BlockSpec((1, 1, blk), lambda l, j: (l, 0, j)),
        ],
        out_specs=pl.BlockSpec((1, 8, blk), lambda l, j: (l, 0, j)),
        out_shape=jax.ShapeDtypeStruct((DEPTH, 8, ncol), F32),
        compiler_params=_cparams(2),
        name="adaln_mod",
    )(c_pad, w_ada, b_ada.reshape(DEPTH, 1, ncol))


def _rope_kernel(pos_ref, freq_ref, sign_ref, c_ref, s_ref):
    ang = pos_ref[0].astype(F32) * freq_ref[...]
    c_ref[0] = jnp.cos(ang)
    s_ref[0] = jnp.sin(ang) * sign_ref[...]


def _rope_tables(positions, freq_lanes, sign_lanes):
    tm = 2048
    spec = pl.BlockSpec((1, tm, LANES), lambda b, i: (b, i, 0))
    vec = pl.BlockSpec((1, LANES), lambda b, i: (0, 0))
    return pl.pallas_call(
        _rope_kernel,
        grid=(BATCH, SEQ // tm),
        in_specs=[pl.BlockSpec((1, tm, 1), lambda b, i: (b, i, 0)), vec, vec],
        out_specs=[spec, spec],
        out_shape=[jax.ShapeDtypeStruct((BATCH, SEQ, LANES), F32)] * 2,
        compiler_params=_cparams(2),
        name="rope_tables",
    )(positions.reshape(BATCH, SEQ, 1), freq_lanes, sign_lanes)


def _ssm_prep_kernel(ar_ref, ai_ref, ldt_ref, bre_ref, bim_ref, cre_ref, cim_ref,
                     m_ref, ws_ref, wot_ref, dec_ref):
    ar = ar_ref[0, 0]
    ai = ai_ref[0, 0]
    dt = jnp.exp(ldt_ref[0, 0])
    mag = jnp.exp(ar * dt)
    ab_re = mag * jnp.cos(ai * dt)
    ab_im = mag * jnp.sin(ai * dt)
    den = ar * ar + ai * ai
    nr = ab_re - 1.0
    ni = ab_im
    coef_re = (nr * ar + ni * ai) / den
    coef_im = (ni * ar - nr * ai) / den
    b_re = bre_ref[0, 0]
    b_im = bim_ref[0, 0]
    bb_re = coef_re * b_re - coef_im * b_im
    bb_im = coef_re * b_im + coef_im * b_re
    c_re = cre_ref[0, 0]
    c_im = cim_ref[0, 0]

    pw = [(jnp.ones_like(ab_re), jnp.zeros_like(ab_im))]
    for _ in range(SSM_CHUNK):
        pr, pi = pw[-1]
        pw.append((pr * ab_re - pi * ab_im, pr * ab_im + pi * ab_re))

    bb_cat = jnp.concatenate([bb_re, -bb_im], axis=1)
    r0 = []
    for j in range(SSM_CHUNK + 1):
        pr, pi = pw[j]
        e_re = c_re * pr - c_im * pi
        e_im = c_re * pi + c_im * pr
        if j < SSM_CHUNK:
            r0.append(_dot3(bb_cat, jnp.concatenate([e_re, e_im], axis=1), nt=True).astype(BF16))
            s = SSM_CHUNK - 1 - j
            ws_ref[0, 0, s * LANES:(s + 1) * LANES, :] = jnp.concatenate(
                [bb_re * pr - bb_im * pi, bb_re * pi + bb_im * pr], axis=1).astype(BF16)
        if j >= 1:
            wot_ref[0, 0, (j - 1) * LANES:j * LANES, :] = jnp.concatenate([e_re, -e_im], axis=1).astype(BF16)
    zero = jnp.zeros((LANES, LANES), BF16)
    for s in range(SSM_CHUNK):
        for j in range(SSM_CHUNK):
            m_ref[0, 0, s * LANES:(s + 1) * LANES, j * LANES:(j + 1) * LANES] = r0[j - s] if j >= s else zero
    pr, pi = pw[SSM_CHUNK]
    dec_ref[0, 0] = jnp.concatenate([pr, pi], axis=1)


def _ssm_prepare(a_re, a_im, log_dt, b_re, b_im, c_re, c_im):
    nq, gt, p, m = SSM_QBLOCKS, GROUPS_PER_TILE, SSM_STATE, SSM_GROUP
    eye = jnp.eye(gt, dtype=F32)

    def lanes(v):
        return v.reshape(DEPTH, nq, 1, gt * p)

    def place(v):
        rows = v.shape[3]
        out = v[:, :, :, :, None, :] * eye[None, None, :, None, :, None]
        return out.reshape(DEPTH, nq, gt * rows, gt * p)

    ldt = jnp.broadcast_to(log_dt[:, :, None], (DEPTH, SSM_GROUPS, p))
    bt_re = place(jnp.swapaxes(b_re, 2, 3).reshape(DEPTH, nq, gt, m, p))
    bt_im = place(jnp.swapaxes(b_im, 2, 3).reshape(DEPTH, nq, gt, m, p))
    ct_re = place(c_re.reshape(DEPTH, nq, gt, m, p))
    ct_im = place(c_im.reshape(DEPTH, nq, gt, m, p))

    vec = pl.BlockSpec((1, 1, 1, STATE_LANES), lambda l, q: (l, q, 0, 0))
    mat = pl.BlockSpec((1, 1, LANES, STATE_LANES), lambda l, q: (l, q, 0, 0))
    return pl.pallas_call(
        _ssm_prep_kernel,
        grid=(DEPTH, nq),
        in_specs=[vec, vec, vec, mat, mat, mat, mat],
        out_specs=[
            pl.BlockSpec((1, 1, CHUNK_LANES, CHUNK_LANES), lambda l, q: (l, q, 0, 0)),
            pl.BlockSpec((1, 1, CHUNK_LANES, 2 * STATE_LANES), lambda l, q: (l, q, 0, 0)),
            pl.BlockSpec((1, 1, CHUNK_LANES, 2 * STATE_LANES), lambda l, q: (l, q, 0, 0)),
            pl.BlockSpec((1, 1, 1, 2 * STATE_LANES), lambda l, q: (l, q, 0, 0)),
        ],
        out_shape=[
            jax.ShapeDtypeStruct((DEPTH, nq, CHUNK_LANES, CHUNK_LANES), BF16),
            jax.ShapeDtypeStruct((DEPTH, nq, CHUNK_LANES, 2 * STATE_LANES), BF16),
            jax.ShapeDtypeStruct((DEPTH, nq, CHUNK_LANES, 2 * STATE_LANES), BF16),
            jax.ShapeDtypeStruct((DEPTH, nq, 1, 2 * STATE_LANES), F32),
        ],
        compiler_params=_cparams(2),
        name="ssm_prepare",
    )(lanes(a_re), lanes(a_im), lanes(ldt), bt_re, bt_im, ct_re, ct_im)


SSM_ROWS = 128
SSM_TOK = SSM_ROWS * SSM_CHUNK


SSM_MT = 256
SSM_SUB = 8


def _ssm_kernel(u_ref, m_ref, ws_ref, wot_ref, dec_ref, d_ref, y_ref, s_scr, h_scr, carry_scr):
    r = pl.program_id(1)

    @pl.when(r == 0)
    def _():
        carry_scr[...] = jnp.zeros_like(carry_scr)

    x = jnp.concatenate(
        [jnp.concatenate([u_ref[b, pl.ds(s, SSM_ROWS, stride=SSM_CHUNK), :] for s in range(SSM_CHUNK)], axis=1)
         for b in range(BATCH)], axis=0)
    xb = x.astype(BF16)
    s_scr[...] = _dot(xb, ws_ref[0, 0])

    dec = dec_ref[0, 0]

    def cmul(a, b):
        return a[0] * b[0] - a[1] * b[1], a[0] * b[1] + a[1] * b[0]

    d_pow = [(jnp.ones((1, STATE_LANES), F32), jnp.zeros((1, STATE_LANES), F32)),
             (dec[:, :STATE_LANES], dec[:, STATE_LANES:])]
    for _ in range(2, SSM_SUB + 1):
        d_pow.append(cmul(d_pow[-1], d_pow[1]))
    d_rows = tuple(jnp.concatenate([d_pow[k][part] for k in range(SSM_SUB)], axis=0) for part in range(2))
    row_id = lax.broadcasted_iota(jnp.int32, (SSM_SUB, STATE_LANES), 0)

    def shift_rows(v, k):
        return jnp.where(row_id >= k, pltpu.roll(v, k, axis=0), 0.0)

    def body(i, hs):
        out = []
        for b in range(BATCH):
            base = b * SSM_ROWS + i * SSM_SUB
            blk = s_scr[pl.ds(base, SSM_SUB), :]
            loc = (blk[:, :STATE_LANES], blk[:, STATE_LANES:])
            k = 1
            while k < SSM_SUB:
                add = cmul(d_pow[k], (shift_rows(loc[0], k), shift_rows(loc[1], k)))
                loc = (loc[0] + add[0], loc[1] + add[1])
                k *= 2
            h = (hs[b][:, :STATE_LANES], hs[b][:, STATE_LANES:])
            carried = cmul(d_rows, h)
            h_scr[pl.ds(base, SSM_SUB), :] = jnp.concatenate(
                [shift_rows(loc[0], 1) + carried[0], shift_rows(loc[1], 1) + carried[1]], axis=1)
            nxt = cmul(d_pow[SSM_SUB], h)
            last = slice(SSM_SUB - 1, SSM_SUB)
            out.append(jnp.concatenate([nxt[0] + loc[0][last], nxt[1] + loc[1][last]], axis=1))
        return tuple(out)

    tiles = [(t * SSM_MT, (t + 1) * SSM_MT) for t in range(CHUNK_LANES // SSM_MT)]
    intra = [_dot(xb[:, :hi], m_ref[0, 0, :hi, lo:hi]) for lo, hi in tiles]

    hs = tuple(carry_scr[b] for b in range(BATCH))
    for i in range(SSM_ROWS // SSM_SUB):
        hs = body(i, hs)
    for b in range(BATCH):
        carry_scr[b] = hs[b]

    d_skip = jnp.concatenate([d_ref[0, 0]] * SSM_CHUNK, axis=1)
    y_state = _dot_nt(h_scr[...].astype(BF16), wot_ref[0, 0]) + d_skip * x
    for (lo, hi), y_intra in zip(tiles, intra):
        y_t = y_intra + y_state[:, lo:hi]
        for j in range(lo // LANES, hi // LANES):
            for b in range(BATCH):
                y_ref[b, pl.ds(j, SSM_ROWS, stride=SSM_CHUNK), :] = (
                    y_t[b * SSM_ROWS:(b + 1) * SSM_ROWS, j * LANES - lo:(j + 1) * LANES - lo])


def _ssm_scan(u, m_mat, ws, wot, dec, d_skip, layer):
    nr = SEQ // SSM_TOK
    tok = pl.BlockSpec((BATCH, SSM_TOK, LANES), lambda q, r: (0, r, q))
    return pl.pallas_call(
        _ssm_kernel,
        grid=(SSM_QBLOCKS, nr),
        in_specs=[
            tok,
            pl.BlockSpec((1, 1, CHUNK_LANES, CHUNK_LANES), lambda q, r: (layer, q, 0, 0)),
            pl.BlockSpec((1, 1, CHUNK_LANES, 2 * STATE_LANES), lambda q, r: (layer, q, 0, 0)),
            pl.BlockSpec((1, 1, CHUNK_LANES, 2 * STATE_LANES), lambda q, r: (layer, q, 0, 0)),
            pl.BlockSpec((1, 1, 1, 2 * STATE_LANES), lambda q, r: (layer, q, 0, 0)),
            pl.BlockSpec((1, 1, 1, LANES), lambda q, r: (layer, q, 0, 0)),
        ],
        out_specs=tok,
        out_shape=jax.ShapeDtypeStruct((BATCH, SEQ, SSM_WIDTH), F32),
        scratch_shapes=[
            pltpu.VMEM((BATCH * SSM_ROWS, 2 * STATE_LANES), F32),
            pltpu.VMEM((BATCH * SSM_ROWS, 2 * STATE_LANES), F32),
            pltpu.VMEM((BATCH, 1, 2 * STATE_LANES), F32),
        ],
        compiler_params=_cparams(2),
        name="ssm_scan",
    )(u.reshape(BATCH, SEQ, SSM_WIDTH), m_mat, ws, wot, dec, d_skip).reshape(TOKENS, SSM_WIDTH)


IN_TM = 1024
VT_ROWS = V_DIM + 16
VT_CHUNK = 256
_U0, _CQ0, _CKV0, _KA0, _KB0, _GA0, _GB0, _IN_END = 0, 512, 768, 896, 1024, 1152, 2176, 3200


def _in_kernel(x_ref, mod_ref, g1_ref, w_ref, qg_ref, kvg_ref, wq1_ref, wq2_ref, wk_ref, wvt_ref,
               c_ref, s_ref, u_ref, q_ref, k_ref, vt_ref, ga_ref, gb_ref, *, q_scale):
    x = x_ref[...]
    mod = mod_ref[0]
    sh = mod[:, 0:D_MODEL]
    sc = mod[:, D_MODEL:2 * D_MODEL]
    xn = x * lax.rsqrt(jnp.mean(x * x, axis=-1, keepdims=True) + EPS) * g1_ref[...]
    h = (xn * (1.0 + sc) + sh).astype(BF16)
    z = _dot(h, w_ref[...])

    u_ref[...] = z[:, _U0:_CQ0]
    ga_ref[...] = _sigmoid(z[:, _GA0:_GB0]).astype(BF16)
    gb_ref[...] = _sigmoid(z[:, _GB0:_IN_END]).astype(BF16)

    cos = c_ref[0]
    sin = s_ref[0]
    cq = z[:, _CQ0:_CKV0]
    cqn = (cq * lax.rsqrt(jnp.mean(cq * cq, axis=-1, keepdims=True) + EPS) * qg_ref[...]).astype(BF16)
    qa = _dot(cqn, wq1_ref[...])
    qb = _dot(cqn, wq2_ref[...])
    ckv = z[:, _CKV0:_KA0]
    ckvn = (ckv * lax.rsqrt(jnp.mean(ckv * ckv, axis=-1, keepdims=True) + EPS) * kvg_ref[...]).astype(BF16)
    kn = _dot(ckvn, wk_ref[...])
    vvt = _dot_nt(wvt_ref[...], ckvn)
    k_pe = z[:, _KA0:_KB0] * cos + z[:, _KB0:_GA0] * sin
    for hd in range(N_HEADS):
        sl = slice(hd * HEAD_PAD, (hd + 1) * HEAD_PAD)
        q_ref[0, hd] = ((qa[:, sl] * cos + qb[:, sl] * sin) * q_scale).astype(BF16)
        k_ref[0, hd] = (kn[:, sl] + k_pe).astype(BF16)
        for ch in range(IN_TM // VT_CHUNK):
            lanes = slice(ch * VT_CHUNK, (ch + 1) * VT_CHUNK)
            vt_ref[0, hd, ch, :V_DIM, :] = vvt[hd * V_DIM:(hd + 1) * V_DIM, lanes].astype(BF16)
            vt_ref[0, hd, ch, V_DIM:, :] = jnp.ones((VT_ROWS - V_DIM, VT_CHUNK), BF16)


def _in_proj(x, mod_l, g1, w_all, qg, kvg, wq1, wq2, wk, wvt, cos_t, sin_t):
    tm = IN_TM
    assert ATT_TK % VT_CHUNK == 0 and tm % VT_CHUNK == 0
    nt = SEQ // tm
    const = _resident
    tok = lambda width: pl.BlockSpec((tm, width), lambda i: (i, 0))
    head = pl.BlockSpec((1, N_HEADS, tm, HEAD_PAD), lambda i: (i // nt, 0, i % nt, 0))
    tab = pl.BlockSpec((1, tm, LANES), lambda i: (i // nt, i % nt, 0))
    hshape = jax.ShapeDtypeStruct((BATCH, N_HEADS, SEQ, HEAD_PAD), BF16)
    q_scale = QK_DIM ** -0.5 * math.log2(math.e)
    return pl.pallas_call(
        functools.partial(_in_kernel, q_scale=q_scale),
        grid=(TOKENS // tm,),
        in_specs=[
            tok(D_MODEL),
            pl.BlockSpec((1, 1, N_MOD * D_MODEL), lambda i: (i // nt, 0, 0)),
            const((1, D_MODEL)),
            const((D_MODEL, _IN_END)),
            const((1, Q_LORA)), const((1, KV_LORA)),
            const((Q_LORA, N_HEADS * HEAD_PAD)), const((Q_LORA, N_HEADS * HEAD_PAD)),
            const((KV_LORA, N_HEADS * HEAD_PAD)), const((N_HEADS * V_DIM, KV_LORA)),
            tab, tab,
        ],
        out_specs=[tok(SSM_WIDTH), head, head,
                   pl.BlockSpec((1, N_HEADS, tm // VT_CHUNK, VT_ROWS, VT_CHUNK),
                                lambda i: (i // nt, 0, i % nt, 0, 0)),
                   tok(D_MODEL), tok(D_MODEL)],
        out_shape=[
            jax.ShapeDtypeStruct((TOKENS, SSM_WIDTH), F32), hshape, hshape,
            jax.ShapeDtypeStruct((BATCH, N_HEADS, SEQ // VT_CHUNK, VT_ROWS, VT_CHUNK), BF16),
            jax.ShapeDtypeStruct((TOKENS, D_MODEL), BF16), jax.ShapeDtypeStruct((TOKENS, D_MODEL), BF16),
        ],
        compiler_params=_cparams(1),
        name="in_proj",
    )(x, mod_l, g1, w_all, qg, kvg, wq1, wq2, wk, wvt, cos_t, sin_t)


ATT_TQ = 4096
ATT_TK = 256
ATT_KB_PER_Q = ATT_TQ // ATT_TK
ATT_CW = 256
ATT_NCT = ATT_TQ // ATT_CW


def _attn_kernel(q_ref, k_ref, vt_ref, pcol_ref, prow_ref, o_ref):
    qi = pl.program_id(2)

    def keys(ki):
        return k_ref[0, 0, pl.ds(pl.multiple_of(ki * ATT_TK, ATT_TK), ATT_TK), :]

    def values_t(ki):
        nchunk = ATT_TK // VT_CHUNK
        return jnp.concatenate([vt_ref[0, 0, ki * nchunk + c] for c in range(nchunk)], axis=1)

    def soft(s, m_old):
        m_new = jnp.maximum(m_old, jnp.max(s, axis=0, keepdims=True))
        return m_new, jnp.exp2(m_old - m_new), jnp.exp2(s - m_new).astype(BF16)

    cols = [slice(c * ATT_CW, (c + 1) * ATT_CW) for c in range(ATT_NCT)]

    def score(ki, c):
        return _dot_nt(keys(ki), q_ref[0, 0, cols[c], :])

    def run_stages(stages, m, acc):
        m, acc = list(m), list(acc)
        s_cur = [score(stages[0][0], c) if c >= stages[0][1] else None for c in range(ATT_NCT)]
        for idx, (ki, c0, key_rows) in enumerate(stages):
            nxt = stages[idx + 1] if idx + 1 < len(stages) else None
            stats = {}
            for c in range(c0, ATT_NCT):
                t = s_cur[c]
                if key_rows is not None and c == c0:
                    t = jnp.where(pcol_ref[0, key_rows, :] <= prow_ref[0, :, cols[c]], t, NEG_BIG)
                stats[c] = soft(t, m[c])
            vtb = values_t(ki)
            s_nxt = [None] * ATT_NCT
            for c in range(c0, ATT_NCT):
                if nxt is not None and c >= nxt[1]:
                    s_nxt[c] = score(nxt[0], c)
                m[c] = stats[c][0]
                acc[c] = stats[c][1] * acc[c] + _dot(vtb, stats[c][2])
            s_cur = s_nxt
        return tuple(m), tuple(acc)

    def visible_blocks(i, carry):
        base = ATT_KB_PER_Q * i
        return run_stages([(base + j, 0, None) for j in range(ATT_KB_PER_Q)], *carry)

    init = (tuple(jnp.full((1, ATT_CW), NEG_BIG, F32) for _ in range(ATT_NCT)),
            tuple(jnp.zeros((VT_ROWS, ATT_CW), F32) for _ in range(ATT_NCT)))
    m, acc = lax.fori_loop(0, qi, visible_blocks, init)
    base = ATT_KB_PER_Q * qi
    span = [(base + j, (j * ATT_TK) // ATT_CW, slice(j * ATT_TK, (j + 1) * ATT_TK)) for j in range(ATT_KB_PER_Q)]
    _, acc = run_stages(span, m, acc)
    acc = jnp.concatenate(acc, axis=1)
    o_ref[0, 0] = (acc[:V_DIM] / acc[V_DIM:V_DIM + 1]).astype(BF16)


def _attention(q, k, vt, pos_col, pos_row):
    nq = SEQ // ATT_TQ
    return pl.pallas_call(
        _attn_kernel,
        grid=(BATCH, N_HEADS, nq),
        in_specs=[
            pl.BlockSpec((1, 1, ATT_TQ, HEAD_PAD), lambda b, h, i: (b, h, i, 0)),
            pl.BlockSpec((1, 1, SEQ, HEAD_PAD), lambda b, h, i: (b, h, 0, 0)),
            pl.BlockSpec((1, 1, SEQ // VT_CHUNK, VT_ROWS, VT_CHUNK), lambda b, h, i: (b, h, 0, 0, 0)),
            pl.BlockSpec((1, ATT_TQ, 1), lambda b, h, i: (b, i, 0)),
            pl.BlockSpec((1, 1, ATT_TQ), lambda b, h, i: (b, 0, i)),
        ],
        out_specs=pl.BlockSpec((1, 1, V_DIM, ATT_TQ), lambda b, h, i: (b, h, 0, i)),
        out_shape=jax.ShapeDtypeStruct((BATCH, N_HEADS, V_DIM, SEQ), BF16),
        compiler_params=_cparams(3),
        name="mla_attention",
    )(q, k, vt, pos_col, pos_row)


MIX_TM = 1024


def _gelu_tanh(x):
    return x * (0.5 * (1.0 + jnp.tanh(math.sqrt(2.0 / math.pi) * (x + 0.044715 * (x * x * x)))))


def _mix_kernel(x_ref, y_ref, o_ref, ga_ref, gb_ref, mod_ref, wglu_ref, bglu_ref, wa_ref, wb_ref, wo_ref, out_ref):
    y = _gelu_tanh(y_ref[...])
    gl = _dot(y.astype(BF16), wglu_ref[...]) + bglu_ref[...]
    ya = _dot((y * _sigmoid(gl)).astype(BF16), wa_ref[...])
    ot = jnp.concatenate([o_ref[0, hd] for hd in range(N_HEADS)], axis=0)
    yb = lax.dot_general(ot, wb_ref[...], (((0,), (0,)), ((), ())), preferred_element_type=F32)
    merged = ga_ref[...] * ya + gb_ref[...] * yb
    g1 = mod_ref[0][:, 2 * D_MODEL:3 * D_MODEL]
    out_ref[...] = x_ref[...] + g1 * _dot(merged.astype(BF16), wo_ref[...])


def _mix(x, y_ssm, o, ga, gb, mod_l, wglu, bglu, wa, wb, wo):
    tm = MIX_TM
    nt = SEQ // tm
    const = _resident
    tok = lambda width: pl.BlockSpec((tm, width), lambda i: (i, 0))
    return pl.pallas_call(
        _mix_kernel,
        grid=(TOKENS // tm,),
        in_specs=[
            tok(D_MODEL), tok(SSM_WIDTH),
            pl.BlockSpec((1, N_HEADS, V_DIM, tm), lambda i: (i // nt, 0, 0, i % nt)),
            tok(D_MODEL), tok(D_MODEL),
            pl.BlockSpec((1, 1, N_MOD * D_MODEL), lambda i: (i // nt, 0, 0)),
            const((SSM_WIDTH, SSM_WIDTH)), const((1, SSM_WIDTH)),
            const((SSM_WIDTH, D_MODEL)), const((N_HEADS * V_DIM, D_MODEL)), const((D_MODEL, D_MODEL)),
        ],
        out_specs=tok(D_MODEL),
        out_shape=jax.ShapeDtypeStruct((TOKENS, D_MODEL), F32),
        compiler_params=_cparams(1),
        name="branch_mix",
    )(x, y_ssm, o, ga, gb, mod_l, wglu, bglu, wa, wb, wo)


FFN_TM = 512


def _ffn_kernel(x_ref, mod_ref, g2_ref, wg_ref, wu_ref, wd_ref, fg_ref, out_ref, *, final_norm):
    x = x_ref[...]
    mod = mod_ref[0]
    sh = mod[:, 3 * D_MODEL:4 * D_MODEL]
    sc = mod[:, 4 * D_MODEL:5 * D_MODEL]
    xn = x * lax.rsqrt(jnp.mean(x * x, axis=-1, keepdims=True) + EPS) * g2_ref[...]
    h = (xn * (1.0 + sc) + sh).astype(BF16)
    g = _dot(h, wg_ref[...])
    act = (g * _sigmoid(g)) * _dot(h, wu_ref[...])
    y = x + mod[:, 5 * D_MODEL:6 * D_MODEL] * _dot(act.astype(BF16), wd_ref[...])
    if final_norm:
        y = y * lax.rsqrt(jnp.mean(y * y, axis=-1, keepdims=True) + EPS) * fg_ref[...]
    out_ref[...] = y


def _ffn(x, mod_l, g2, wg, wu, wd, fg, final_norm):
    tm = FFN_TM
    nt = SEQ // tm
    return pl.pallas_call(
        functools.partial(_ffn_kernel, final_norm=final_norm),
        grid=(TOKENS // tm,),
        in_specs=[
            pl.BlockSpec((tm, D_MODEL), lambda i: (i, 0)),
            pl.BlockSpec((1, 1, N_MOD * D_MODEL), lambda i: (i // nt, 0, 0)),
            _resident((1, D_MODEL)),
            _resident((D_MODEL, D_FF)), _resident((D_MODEL, D_FF)), _resident((D_FF, D_MODEL)),
            _resident((1, D_MODEL)),
        ],
        out_specs=pl.BlockSpec((tm, D_MODEL), lambda i: (i, 0)),
        out_shape=jax.ShapeDtypeStruct((TOKENS, D_MODEL), F32),
        compiler_params=_cparams(1),
        name="swiglu_ffn",
    )(x, mod_l, g2, wg, wu, wd, fg)


def _pad_heads(w, dim):
    k = w.shape[0]
    w = w.reshape(k, N_HEADS, dim)
    return jnp.pad(w, ((0, 0), (0, 0), (0, HEAD_PAD - dim))).reshape(k, N_HEADS * HEAD_PAD)


def _swap_rope_halves(w_rope):
    half = QK_ROPE // 2
    return jnp.concatenate([w_rope[..., half:], w_rope[..., :half]], axis=-1)


def kernel(x, c, positions, w_ada, b_ada, norm1_g, w_in, ssm_a_re, ssm_a_im, ssm_log_dt, ssm_b_re, ssm_b_im,
           ssm_c_re, ssm_c_im, ssm_d, w_glu, b_glu, w_a_out, q_norm_g, w_uq, kv_norm_g, w_uk, w_uv, w_b_out,
           w_out, norm2_g, w_gate, w_up, w_down, final_g):
    inv_freq = ROPE_BASE ** (-jnp.arange(0, QK_ROPE, 2, dtype=F32) / QK_ROPE)
    half = QK_ROPE // 2
    freq_lanes = jnp.zeros((1, LANES), F32).at[0, QK_NOPE:QK_DIM].set(jnp.tile(inv_freq, 2))
    sign_lanes = (jnp.zeros((1, LANES), F32).at[0, QK_NOPE:QK_NOPE + half].set(-1.0)
                  .at[0, QK_NOPE + half:QK_DIM].set(1.0))
    cos_t, sin_t = _rope_tables(positions, freq_lanes, sign_lanes)
    pos_col = positions.reshape(BATCH, SEQ, 1)
    pos_row = positions.reshape(BATCH, 1, SEQ)

    mods = _adaln_mods(jnp.pad(c, ((0, 8 - BATCH), (0, 0))), w_ada, b_ada)
    m_mat, ws, wot, dec = _ssm_prepare(ssm_a_re, ssm_a_im, ssm_log_dt, ssm_b_re, ssm_b_im, ssm_c_re, ssm_c_im)
    d_skip = ssm_d.reshape(DEPTH, SSM_QBLOCKS, 1, LANES)

    xt = x.reshape(TOKENS, D_MODEL)
    for l in range(DEPTH):
        mod_l = mods[l, :BATCH].reshape(BATCH, 1, N_MOD * D_MODEL)
        wi = w_in[l]
        o_u, o_cq, o_ckv, o_kr, o_ga = (SSM_WIDTH, SSM_WIDTH + Q_LORA, SSM_WIDTH + Q_LORA + KV_LORA,
                                         SSM_WIDTH + Q_LORA + KV_LORA + QK_ROPE,
                                         SSM_WIDTH + Q_LORA + KV_LORA + QK_ROPE + D_MODEL)
        w_kr = wi[:, o_ckv:o_kr]
        rope_pad = ((0, 0), (QK_NOPE, HEAD_PAD - QK_DIM))
        w_all = jnp.concatenate([
            wi[:, :o_ckv],
            jnp.pad(w_kr, rope_pad), jnp.pad(_swap_rope_halves(w_kr), rope_pad),
            wi[:, o_kr:],
        ], axis=1).astype(BF16)
        uq = w_uq[l].reshape(Q_LORA, N_HEADS, QK_DIM)
        uq_swapped = jnp.concatenate([jnp.zeros_like(uq[..., :QK_NOPE]), _swap_rope_halves(uq[..., QK_NOPE:])], axis=-1)
        wq1 = _pad_heads(w_uq[l], QK_DIM).astype(BF16)
        wq2 = _pad_heads(uq_swapped.reshape(Q_LORA, N_HEADS * QK_DIM), QK_DIM).astype(BF16)
        wk = _pad_heads(w_uk[l], QK_NOPE).astype(BF16)
        wvt = w_uv[l].T.astype(BF16)

        u, q, k, vt, ga, gb = _in_proj(xt, mod_l, norm1_g[l].reshape(1, D_MODEL), w_all,
                                      q_norm_g[l].reshape(1, Q_LORA), kv_norm_g[l].reshape(1, KV_LORA),
                                      wq1, wq2, wk, wvt, cos_t, sin_t)
        y_ssm = _ssm_scan(u, m_mat, ws, wot, dec, d_skip, l)
        o = _attention(q, k, vt, pos_col, pos_row)
        xt = _mix(xt, y_ssm, o, ga, gb, mod_l, w_glu[l].astype(BF16), b_glu[l].reshape(1, SSM_WIDTH),
                  w_a_out[l].astype(BF16), w_b_out[l].astype(BF16), w_out[l].astype(BF16))
        xt = _ffn(xt, mod_l, norm2_g[l].reshape(1, D_MODEL), w_gate[l].astype(BF16), w_up[l].astype(BF16),
                  w_down[l].astype(BF16), final_g.reshape(1, D_MODEL), final_norm=(l == DEPTH - 1))
    return xt.reshape(BATCH, SEQ, D_MODEL)
```
